```python
import jax, jax.numpy as jnp
from jax import lax
import numpy as np

D_MODEL = 1024
BATCH = 8
SEQ = 4096
DEPTH = 1
DEC_BATCH = 32
DEC_SEQ = 1
PAST_LEN = 16384
PAGE_SIZE = 128

N_META = 16
SB_HEADS = 8
SB_HEAD_DIM = 64
SB_WIDTH = SB_HEADS * SB_HEAD_DIM
SB_BLOCK = 128
SB_SCALE = SB_HEAD_DIM ** -0.5
SB_BIAS_INIT = -6.0
HG_HEADS = 4
HG_DK = 128
HG_DV = 128
HG_KW = HG_HEADS * HG_DK
HG_VW = HG_HEADS * HG_DV
HG_CHUNK = 64
HG_SCALE = HG_DK ** -0.5
MIX_WIDTH = SB_WIDTH + HG_VW
IN_WIDTH = 3 * SB_WIDTH + 2 * HG_KW + 2 * HG_VW
SPLITS = (SB_WIDTH, 2 * SB_WIDTH, 3 * SB_WIDTH, 3 * SB_WIDTH + HG_KW,
          3 * SB_WIDTH + 2 * HG_KW, 3 * SB_WIDTH + 2 * HG_KW + HG_VW)
PEER_HEADS = 8
PEER_N_KEYS = 128
PEER_N_EXPERTS = PEER_N_KEYS * PEER_N_KEYS
PEER_QDIM = 256
PEER_HALF = PEER_QDIM // 2
PEER_TOPK = 16
PEER_BLOCK = 128
DN_ALPHA = (2.0 * DEPTH) ** 0.25
DN_BETA = (8.0 * DEPTH) ** -0.25
EPS = 1e-5

kernel_name = 'hymba_sb_hgrn2_peer_step'

F32 = jnp.float32


def layer_norm(x, g, b):
    xf = x.astype(F32)
    mu = jnp.mean(xf, -1, keepdims=True)
    xc = xf - mu
    var = jnp.mean(xc * xc, -1, keepdims=True)
    return xc * lax.rsqrt(var + EPS) * g.astype(F32) + b.astype(F32)


def rms_norm(x, g):
    xf = x.astype(F32)
    return xf * lax.rsqrt(jnp.mean(xf * xf, -1, keepdims=True) + EPS) * g.astype(F32)


def post_ln(x, delta, g, b, dtype):
    return layer_norm(DN_ALPHA * x.astype(F32) + delta.astype(F32), g, b).astype(dtype)


def in_proj(xn, w):
    p = jnp.einsum('bld,de->ble', xn, w)
    b, l, _ = p.shape
    q_sb, k_sb, v_sb, q_hg, f_hg, i_hg, g_hg = jnp.split(p, SPLITS, axis=-1)
    sbh = lambda a: a.reshape(b, l, SB_HEADS, SB_HEAD_DIM)
    hgh = lambda a, d: jnp.swapaxes(a.reshape(b, l, HG_HEADS, d), 1, 2).astype(F32)
    return (sbh(q_sb), sbh(k_sb), sbh(v_sb), hgh(q_hg, HG_DK), hgh(f_hg, HG_DK),
            hgh(i_hg, HG_DV), g_hg)


def hgrn_gates(q_raw, f_raw, lb):
    lb = lb.reshape(HG_HEADS, 1, HG_DK)
    log_f = jnp.logaddexp(jnp.log(lb), jnp.log1p(-lb) + jax.nn.log_sigmoid(f_raw))
    k = (1.0 - lb) * jax.nn.sigmoid(-f_raw)
    q = jax.nn.silu(q_raw) * HG_SCALE
    return q, k, log_f


def hgrn_chunk(S, q, k, log_f, v):
    c = q.shape[2]
    G = jnp.cumsum(log_f, axis=2)
    causal = jnp.tril(jnp.ones((c, c), bool))[:, :, None]
    diff = G[:, :, :, None, :] - G[:, :, None, :, :]
    decay = jnp.exp(jnp.where(causal, diff, -jnp.inf))
    scores = jnp.einsum('bhtk,bhsk,bhtsk->bhts', q, k, decay)
    o = (jnp.einsum('bhts,bhsv->bhtv', scores, v)
         + jnp.einsum('bhtk,bhkv->bhtv', q * jnp.exp(G), S))
    G_last = G[:, :, -1:, :]
    S_new = (jnp.exp(G_last[:, :, 0, :, None]) * S
             + jnp.einsum('bhsk,bhsv->bhkv', k * jnp.exp(G_last - G), v))
    return o, S_new


def hgrn_prompt(q, k, log_f, v):
    b = q.shape[0]
    S0 = jnp.zeros((b, HG_HEADS, HG_DK, HG_DV), F32)
    o_meta, S = hgrn_chunk(S0, q[:, :, :N_META], k[:, :, :N_META], log_f[:, :, :N_META], v[:, :, :N_META])

    def to_chunks(a):
        a = a[:, :, N_META:]
        bb, hh, ll, dd = a.shape
        return jnp.moveaxis(a.reshape(bb, hh, ll // HG_CHUNK, HG_CHUNK, dd), 2, 0)

    def body(S, xs):
        qc, kc, fc, vc = xs
        o, S = hgrn_chunk(S, qc, kc, fc, vc)
        return S, o

    S, o_real = lax.scan(body, S, (to_chunks(q), to_chunks(k), to_chunks(log_f), to_chunks(v)))
    o_real = jnp.moveaxis(o_real, 0, 2).reshape(b, HG_HEADS, SEQ, HG_DV)
    return jnp.concatenate([o_meta, o_real], axis=2), S


def sb_attend(q, k, v, bias, pos_q, pos_k):
    z = (jnp.einsum('bqhd,bkhd->bhqk', q.astype(F32), k.astype(F32)) * SB_SCALE
         + bias.astype(F32)[None, :, None, None])
    mask = pos_k[None, :] < pos_q[:, None]
    sp = jnp.where(mask, jax.nn.softplus(z), 0.0)
    after = lax.cumsum(sp, axis=3, reverse=True) - sp
    w = jnp.where(mask, jnp.exp(jax.nn.log_sigmoid(z) - after), 0.0)
    return jnp.einsum('bhqk,bkhd->bqhd', w, v.astype(F32))


def sb_prompt(q, k, v, bias):
    b = q.shape[0]
    pos = jnp.arange(N_META + SEQ)
    o_meta = sb_attend(q[:, :N_META], k[:, :N_META], v[:, :N_META], bias, pos[:N_META], pos[:N_META])
    nb = SEQ // SB_BLOCK
    q_blocks = jnp.moveaxis(q[:, N_META:].reshape(b, nb, SB_BLOCK, SB_HEADS, SB_HEAD_DIM), 1, 0)

    def blk(args):
        qb, bi = args
        pos_q = N_META + bi * SB_BLOCK + jnp.arange(SB_BLOCK)
        return sb_attend(qb, k, v, bias, pos_q, pos)

    o_real = lax.map(blk, (q_blocks, jnp.arange(nb)))
    o_real = jnp.moveaxis(o_real, 0, 1).reshape(b, SEQ, SB_HEADS, SB_HEAD_DIM)
    return jnp.concatenate([o_meta, o_real], axis=1)


def mix_out(o_sb, o_hg, g_hg, sb_g, hg_g, w_out):
    b, l = o_sb.shape[:2]
    sb = rms_norm(o_sb.reshape(b, l, SB_WIDTH), sb_g)
    hg = rms_norm(jnp.swapaxes(o_hg, 1, 2), hg_g.reshape(HG_HEADS, HG_DV)).reshape(b, l, HG_VW)
    hg = hg * jax.nn.silu(g_hg.astype(F32))
    return jnp.einsum('ble,ed->bld', jnp.concatenate([sb, hg], axis=-1), w_out.astype(F32))


def peer_block(h, w_pq, sub_keys, expert_u, expert_v):
    n = h.shape[0]
    q = jnp.einsum('nd,de->ne', h.astype(F32), w_pq.astype(F32)).reshape(n, PEER_HEADS, 2, PEER_HALF)
    q = q - jnp.mean(q, -1, keepdims=True)
    q = q * lax.rsqrt(jnp.mean(q * q, -1, keepdims=True) + EPS)
    s = jnp.einsum('nhpc,hpkc->nhpk', q, sub_keys.astype(F32))
    s_top, i_top = lax.top_k(s, PEER_TOPK)
    cand = s_top[:, :, 0, :, None] + s_top[:, :, 1, None, :]
    sc, ci = lax.top_k(cand.reshape(n, PEER_HEADS, PEER_TOPK * PEER_TOPK), PEER_TOPK)
    i1 = jnp.take_along_axis(i_top[:, :, 0], ci // PEER_TOPK, axis=-1)
    i2 = jnp.take_along_axis(i_top[:, :, 1], ci % PEER_TOPK, axis=-1)
    e = i1 * PEER_N_KEYS + i2
    gate = jax.nn.softmax(sc, axis=-1)
    act = jax.nn.gelu(jnp.einsum('nd,nhkd->nhk', h.astype(F32), expert_u[e].astype(F32)), approximate=False)
    return jnp.einsum('nhk,nhkd->nd', gate * act, expert_v[e].astype(F32))


def peer(h, w_pq, sub_keys, expert_u, expert_v):
    shape = h.shape
    h2 = h.reshape(-1, D_MODEL)
    n = h2.shape[0]
    nb = -(-n // PEER_BLOCK)
    hp = jnp.pad(h2, ((0, nb * PEER_BLOCK - n), (0, 0)))
    out = lax.map(lambda hb: peer_block(hb, w_pq, sub_keys, expert_u, expert_v),
                  hp.reshape(nb, PEER_BLOCK, D_MODEL))
    return out.reshape(nb * PEER_BLOCK, D_MODEL)[:n].reshape(shape)


def setup_inputs(seed: int = 0) -> dict:
    key = jax.random.key(seed)
    ks = jax.random.split(key, 24)
    n_pages = PAST_LEN // PAGE_SIZE
    n_used = DEC_BATCH * n_pages
    n_phys = (5 * n_used) // 4

    def nrm(k, shape, scale):
        return jax.random.normal(k, shape, F32) * scale

    x_prompt = nrm(ks[0], (BATCH, SEQ, D_MODEL), 1.0)
    x_sample = nrm(ks[1], (DEC_BATCH, DEC_SEQ, D_MODEL), 1.0)
    cache_k = nrm(ks[2], (DEPTH, n_phys, PAGE_SIZE, SB_HEADS, SB_HEAD_DIM), 1.0)
    cache_v = nrm(ks[3], (DEPTH, n_phys, PAGE_SIZE, SB_HEADS, SB_HEAD_DIM), 1.0)
    state_hgrn = nrm(ks[4], (DEPTH, DEC_BATCH, HG_HEADS, HG_DK, HG_DV), 0.5)
    page_table = jax.random.permutation(ks[5], n_phys)[:n_used].reshape(DEC_BATCH, n_pages).astype(jnp.int32)
    meta_tokens = nrm(ks[6], (N_META, D_MODEL), 1.0)
    emb_ln_g = 1.0 + nrm(ks[7], (D_MODEL,), 0.02)
    emb_ln_b = nrm(ks[8], (D_MODEL,), 0.02)
    w_in = nrm(ks[9], (DEPTH, D_MODEL, IN_WIDTH), D_MODEL ** -0.5)
    w_in = w_in.at[:, :, 2 * SB_WIDTH:3 * SB_WIDTH].multiply(DN_BETA)
    sb_bias = SB_BIAS_INIT + nrm(ks[22], (DEPTH, SB_HEADS), 0.1)
    lb_param = nrm(ks[10], (DEPTH + 1, HG_KW), 0.1)
    sb_norm_g = 1.0 + nrm(ks[11], (DEPTH, SB_WIDTH), 0.02)
    hg_norm_g = 1.0 + nrm(ks[12], (DEPTH, HG_VW), 0.02)
    w_out = nrm(ks[13], (DEPTH, MIX_WIDTH, D_MODEL), MIX_WIDTH ** -0.5 * DN_BETA)
    ln1_g = 1.0 + nrm(ks[14], (DEPTH, D_MODEL), 0.02)
    ln1_b = nrm(ks[15], (DEPTH, D_MODEL), 0.02)
    w_pq = nrm(ks[16], (DEPTH, D_MODEL, PEER_HEADS * PEER_QDIM), D_MODEL ** -0.5)
    peer_sub_keys = nrm(ks[17], (DEPTH, PEER_HEADS, 2, PEER_N_KEYS, PEER_HALF), PEER_HALF ** -0.5)
    peer_u = nrm(ks[18], (DEPTH, PEER_N_EXPERTS, D_MODEL), D_MODEL ** -0.5)
    peer_v = nrm(ks[19], (DEPTH, PEER_N_EXPERTS, D_MODEL), DN_BETA)
    ln2_g = 1.0 + nrm(ks[20], (DEPTH, D_MODEL), 0.02)
    ln2_b = nrm(ks[21], (DEPTH, D_MODEL), 0.02)
    return {'x_prompt': x_prompt, 'x_sample': x_sample, 'cache_k': cache_k, 'cache_v': cache_v,
            'state_hgrn': state_hgrn, 'page_table': page_table, 'meta_tokens': meta_tokens,
            'emb_ln_g': emb_ln_g, 'emb_ln_b': emb_ln_b, 'w_in': w_in, 'sb_bias': sb_bias,
            'lb_param': lb_param, 'sb_norm_g': sb_norm_g, 'hg_norm_g': hg_norm_g, 'w_out': w_out,
            'ln1_g': ln1_g, 'ln1_b': ln1_b, 'w_pq': w_pq, 'peer_sub_keys': peer_sub_keys,
            'peer_u': peer_u, 'peer_v': peer_v, 'ln2_g': ln2_g, 'ln2_b': ln2_b}


def reference(x_prompt, x_sample, cache_k, cache_v, state_hgrn, page_table, meta_tokens,
              emb_ln_g, emb_ln_b, w_in, sb_bias, lb_param, sb_norm_g, hg_norm_g, w_out, ln1_g,
              ln1_b, w_pq, peer_sub_keys, peer_u, peer_v, ln2_g, ln2_b):
    dt = x_prompt.dtype
    b = x_prompt.shape[0]
    db, t = x_sample.shape[0], x_sample.shape[1]
    n_past = page_table.shape[1] * cache_k.shape[2]
    lb_all = jnp.cumsum(jax.nn.softmax(lb_param.astype(F32), axis=0), axis=0)

    meta = jnp.broadcast_to(meta_tokens.astype(dt)[None], (b, N_META, D_MODEL))
    xp = layer_norm(jnp.concatenate([meta, x_prompt], axis=1), emb_ln_g, emb_ln_b).astype(dt)
    xs = layer_norm(x_sample, emb_ln_g, emb_ln_b).astype(x_sample.dtype)

    kp_rows, vp_rows, sp_rows, ks_rows, vs_rows, ss_rows = [], [], [], [], [], []
    pos_q_s = n_past + jnp.arange(t)
    pos_k_s = jnp.arange(n_past + t)
    for l in range(DEPTH):
        q_sb, k_sb, v_sb, q_hg, f_hg, i_hg, g_hg = in_proj(xp, w_in[l])
        q_h, k_h, lf_h = hgrn_gates(q_hg, f_hg, lb_all[l])
        o_sb = sb_prompt(q_sb, k_sb, v_sb, sb_bias[l])
        o_hg, s_p = hgrn_prompt(q_h, k_h, lf_h, i_hg)
        hp = post_ln(xp, mix_out(o_sb, o_hg, g_hg, sb_norm_g[l], hg_norm_g[l], w_out[l]),
                     ln1_g[l], ln1_b[l], dt)
        if l == DEPTH - 1:
            hp = hp[:, N_META:]
        xp = post_ln(hp, peer(hp, w_pq[l], peer_sub_keys[l], peer_u[l], peer_v[l]),
                     ln2_g[l], ln2_b[l], dt)
        kp_rows.append(k_sb)
        vp_rows.append(v_sb)
        sp_rows.append(s_p.astype(state_hgrn.dtype))

        q_sb, k_sb, v_sb, q_hg, f_hg, i_hg, g_hg = in_proj(xs, w_in[l])
        q_h, k_h, lf_h = hgrn_gates(q_hg, f_hg, lb_all[l])
        past_k = cache_k[l][page_table].reshape(db, n_past, SB_HEADS, SB_HEAD_DIM)
        past_v = cache_v[l][page_table].reshape(db, n_past, SB_HEADS, SB_HEAD_DIM)
        k_all = jnp.concatenate([past_k, k_sb.astype(past_k.dtype)], axis=1)
        v_all = jnp.concatenate([past_v, v_sb.astype(past_v.dtype)], axis=1)
        o_sb = sb_attend(q_sb, k_all, v_all, sb_bias[l], pos_q_s, pos_k_s)
        o_hg, s_s = hgrn_chunk(state_hgrn[l].astype(F32), q_h, k_h, lf_h, i_hg)
        hs = post_ln(xs, mix_out(o_sb, o_hg, g_hg, sb_norm_g[l], hg_norm_g[l], w_out[l]),
                     ln1_g[l], ln1_b[l], xs.dtype)
        xs = post_ln(hs, peer(hs, w_pq[l], peer_sub_keys[l], peer_u[l], peer_v[l]),
                     ln2_g[l], ln2_b[l], xs.dtype)
        ks_rows.append(k_sb)
        vs_rows.append(v_sb)
        ss_rows.append(s_s.astype(state_hgrn.dtype))

    y_prompt = xp
    y_sample = xs
    k_prompt = jnp.stack(kp_rows, axis=0)
    v_prompt = jnp.stack(vp_rows, axis=0)
    s_prompt = jnp.stack(sp_rows, axis=0)
    k_sample = jnp.stack(ks_rows, axis=0)
    v_sample = jnp.stack(vs_rows, axis=0)
    s_sample = jnp.stack(ss_rows, axis=0)
    return (y_prompt, y_sample, k_prompt, v_prompt, s_prompt, k_sample, v_sample, s_sample)
```

```python
import functools

import jax
import jax.numpy as jnp
from jax import lax
from jax.experimental import pallas as pl
from jax.experimental.pallas import tpu as pltpu

F32 = jnp.float32
BF16 = jnp.bfloat16

N_META = 16
SB_HEADS = 8
SB_HEAD_DIM = 64
SB_WIDTH = SB_HEADS * SB_HEAD_DIM
SB_SCALE = SB_HEAD_DIM ** -0.5
HG_HEADS = 4
HG_DK = 128
HG_DV = 128
HG_W = HG_HEADS * HG_DK
HG_SCALE = HG_DK ** -0.5
HG_BLOCK = 16
PEER_HEADS = 8
PEER_N_KEYS = 128
PEER_HALF = 128
PEER_TOPK = 16
PEER_SEL = PEER_HEADS * PEER_TOPK
DEPTH = 1
DN_ALPHA = (2.0 * DEPTH) ** 0.25
EPS = 1e-5

LANES = 128
SUBLANES = 8
VMEM_LIMIT = 56 * 1024 * 1024


def _cparams(sem):
    return pltpu.CompilerParams(dimension_semantics=sem, vmem_limit_bytes=VMEM_LIMIT)


def _split2(x):
    hi = x.astype(BF16)
    lo = (x - hi.astype(F32)).astype(BF16)
    return hi, lo


def _split3(x):
    hi = x.astype(BF16)
    r = x - hi.astype(F32)
    mid = r.astype(BF16)
    lo = (r - mid.astype(F32)).astype(BF16)
    return hi, mid, lo


def _dot(a, b):
    return jnp.dot(a, b, preferred_element_type=F32)


def _dot_nt(a, b):
    return lax.dot_general(a, b, (((1,), (1,)), ((), ())), preferred_element_type=F32)


def _dot_tn(a, b):
    return lax.dot_general(a, b, (((0,), (0,)), ((), ())), preferred_element_type=F32)


def _sigmoid(x):
    return 1.0 / (1.0 + jnp.exp(-x))


def _layer_norm(x, g, b):
    mu = jnp.mean(x, -1, keepdims=True)
    xc = x - mu
    var = jnp.mean(xc * xc, -1, keepdims=True)
    return xc * lax.rsqrt(var + EPS) * g + b


def _ln_inproj_body(x_ref, g_ref, b_ref, w_ref, lbp_ref, bd_ref,
                    xn_ref, q_ref, k_ref, v_ref, qh_ref, kh_ref, gc_ref, i_ref, gh_ref):
    xn = _layer_norm(x_ref[...], g_ref[...], b_ref[...])
    xn_ref[...] = xn
    xb = xn.astype(BF16)

    def proj(c0, width):
        return _dot(xb, w_ref[:, c0:c0 + width])

    q_ref[...] = proj(0, SB_WIDTH)
    k_ref[...] = proj(SB_WIDTH, SB_WIDTH)
    v_ref[...] = proj(2 * SB_WIDTH, SB_WIDTH)
    c = 3 * SB_WIDTH
    q_raw = proj(c, HG_W)
    f_raw = proj(c + HG_W, HG_W)
    i_ref[...] = proj(c + 2 * HG_W, HG_W)
    gh_ref[...] = proj(c + 3 * HG_W, HG_W)

    p = lbp_ref[...]
    e = jnp.exp(p - jnp.max(p, 0, keepdims=True))
    lb = e[0:1] / jnp.sum(e, 0, keepdims=True)
    log_f = jnp.log(lb + (1.0 - lb) * _sigmoid(f_raw))
    kh_ref[...] = (1.0 - lb) * _sigmoid(-f_raw)
    qh_ref[...] = q_raw * _sigmoid(q_raw) * HG_SCALE
    bd = bd_ref[...]
    hi, mid, lo = _split3(log_f)
    gc_ref[...] = _dot(bd, hi) + _dot(bd, mid) + _dot(bd, lo)


def _ln_inproj(x, g, b, w_bf, lbp, bd, tm):
    n, d = x.shape
    e = w_bf.shape[1]
    row = lambda w: pl.BlockSpec((tm, w), lambda i: (i, 0))
    full = lambda a: pl.BlockSpec(a.shape, lambda i: (0,) * a.ndim)
    outs = [jax.ShapeDtypeStruct((n, d), F32)] + [jax.ShapeDtypeStruct((n, SB_WIDTH), F32)] * 8
    return pl.pallas_call(
        _ln_inproj_body,
        grid=(n // tm,),
        in_specs=[row(d), full(g), full(b), full(w_bf), full(lbp), full(bd)],
        out_specs=[row(d)] + [row(SB_WIDTH)] * 8,
        out_shape=outs,
        compiler_params=_cparams(("arbitrary",)),
        name="ln_inproj",
    )(x, g, b, w_bf, lbp, bd)


def _block_tril(n, blocks):
    i = jnp.arange(n)
    same = blocks[:, None] == blocks[None, :]
    return (same & (i[None, :] <= i[:, None])).astype(BF16)


def _softplus(z):
    return jnp.maximum(z, 0.0) + jnp.log1p(jnp.exp(-jnp.abs(z)))


def _sb_tile(qh, kt, vt, bias, tri, carry, acc, mask):
    z = _dot_nt(qh, kt.astype(BF16)) + bias
    sp_full = _softplus(z)
    sp = sp_full if mask is None else jnp.where(mask, sp_full, 0.0)
    hi, lo = _split2(sp)
    after_in = _dot(hi, tri) + _dot(lo, tri)
    w = jnp.exp(z - sp_full - after_in - carry)
    if mask is not None:
        w = jnp.where(mask, w, 0.0)
    acc = acc + _dot(w.astype(BF16), vt.astype(BF16))
    carry = carry + after_in[:, 0:1] + sp[:, 0:1]
    return carry, acc


def _sb_prompt_body(bias_ref, q_ref, k_ref, v_ref, km_ref, vm_ref, tri_ref, o_ref, *, tq):
    hp = pl.program_id(1)
    qi = pl.program_id(2)
    tri = tri_ref[...]
    tri_m = tri_ref[0:N_META, 0:N_META]
    row = lax.broadcasted_iota(jnp.int32, (tq, tq), 0)
    col = lax.broadcasted_iota(jnp.int32, (tq, tq), 1)
    diag_mask = col < row
    d = SB_HEAD_DIM
    for hh in range(LANES // d):
        cs = slice(hh * d, (hh + 1) * d)
        bias = bias_ref[hp * (LANES // d) + hh]
        qh = (q_ref[:, cs] * SB_SCALE).astype(BF16)
        carry = jnp.zeros((tq, 1), F32)
        acc = jnp.zeros((tq, d), F32)
        off = pl.multiple_of(qi * tq, tq)
        carry, acc = _sb_tile(qh, k_ref[pl.ds(off, tq), cs], v_ref[pl.ds(off, tq), cs],
                              bias, tri, carry, acc, diag_mask)

        def body(it, c):
            o2 = pl.multiple_of((qi - 1 - it) * tq, tq)
            return _sb_tile(qh, k_ref[pl.ds(o2, tq), cs], v_ref[pl.ds(o2, tq), cs],
                            bias, tri, c[0], c[1], None)

        carry, acc = lax.fori_loop(0, qi, body, (carry, acc))
        carry, acc = _sb_tile(qh, km_ref[:, cs], vm_ref[:, cs], bias, tri_m, carry, acc, None)
        o_ref[:, cs] = acc


def _sb_prompt(q, k, v, km, vm, bias, batch, seq, tq):
    n = q.shape[0]
    nq = seq // tq
    j = jnp.arange(tq)
    tri = (j[:, None] > j[None, :]).astype(BF16)
    qspec = pl.BlockSpec((tq, LANES), lambda b, h, i: (b * nq + i, h))
    kspec = pl.BlockSpec((seq, LANES), lambda b, h, i: (b, h))
    mspec = pl.BlockSpec((N_META, LANES), lambda b, h, i: (0, h))
    return pl.pallas_call(
        functools.partial(_sb_prompt_body, tq=tq),
        grid=(batch, SB_WIDTH // LANES, nq),
        in_specs=[pl.BlockSpec(memory_space=pltpu.SMEM), qspec, kspec, kspec, mspec, mspec,
                  pl.BlockSpec((tq, tq), lambda b, h, i: (0, 0))],
        out_specs=qspec,
        out_shape=jax.ShapeDtypeStruct((n, SB_WIDTH), F32),
        compiler_params=_cparams(("arbitrary", "arbitrary", "arbitrary")),
        name="sb_prompt",
    )(bias, q, k, v, km, vm, tri)


def _sb_decode_body(pt_ref, q_ref, k_ref, v_ref, bias_ref, seg_ref, segt_ref, trit_ref,
                    o_ref, acc_ref, carry_ref):
    p = pl.program_id(1)

    @pl.when(p == 0)
    def _():
        acc_ref[...] = jnp.zeros_like(acc_ref)
        carry_ref[...] = jnp.zeros_like(carry_ref)

    prod = k_ref[...] * (q_ref[0] * SB_SCALE)
    z = _dot(prod.astype(BF16), seg_ref[...]) + bias_ref[...]
    sp = _softplus(z)
    hi, lo = _split2(sp)
    trit = trit_ref[...]
    after_in = _dot(trit, hi) + _dot(trit, lo)
    w = jnp.exp(z - sp - after_in - carry_ref[...])
    wexp = _dot(w.astype(BF16), segt_ref[...])
    acc_ref[...] += jnp.sum(wexp * v_ref[...], axis=0, keepdims=True)
    carry_ref[...] += after_in[0:1, :] + sp[0:1, :]

    @pl.when(p == pl.num_programs(1) - 1)
    def _():
        o_ref[0] = acc_ref[...]


def _sb_decode(q, cache_k, cache_v, page_table, bias):
    db = q.shape[0]
    n_pages = page_table.shape[1]
    n_phys, page = cache_k.shape[0], cache_k.shape[1]
    ck = cache_k.reshape(n_phys, page, SB_WIDTH)
    cv = cache_v.reshape(n_phys, page, SB_WIDTH)
    head_of = jnp.arange(SB_WIDTH) // SB_HEAD_DIM
    seg = (head_of[:, None] == jnp.arange(LANES)[None, :]).astype(BF16)
    segt = seg.T
    j = jnp.arange(page)
    trit = (j[None, :] > j[:, None]).astype(BF16)
    bias_row = jnp.zeros((1, LANES), F32).at[0, :SB_HEADS].set(bias)
    pspec = pl.BlockSpec((None, page, SB_WIDTH),
                         lambda b, p, pt: (pt[b * n_pages + n_pages - 1 - p], 0, 0))
    qspec = pl.BlockSpec((1, 1, SB_WIDTH), lambda b, p, pt: (b, 0, 0))
    full = lambda a: pl.BlockSpec(a.shape, lambda b, p, pt: (0,) * a.ndim)
    out = pl.pallas_call(
        _sb_decode_body,
        grid_spec=pltpu.PrefetchScalarGridSpec(
            num_scalar_prefetch=1,
            grid=(db, n_pages),
            in_specs=[qspec, pspec, pspec, full(bias_row), full(seg), full(segt), full(trit)],
            out_specs=qspec,
            scratch_shapes=[pltpu.VMEM((1, SB_WIDTH), F32), pltpu.VMEM((1, LANES), F32)],
        ),
        out_shape=jax.ShapeDtypeStruct((db, 1, SB_WIDTH), F32),
        compiler_params=_cparams(("arbitrary", "arbitrary")),
        name="sb_decode",
    )(page_table.reshape(-1), q.reshape(db, 1, SB_WIDTH), ck, cv, bias_row, seg, segt, trit)
    return out.reshape(db, SB_WIDTH)


def _hgrn_prompt_body(km_ref, gm_ref, vm_ref, q_ref, k_ref, g_ref, v_ref, ones_ref,
                      o_ref, s_ref, st_ref, *, tb):
    t = HG_BLOCK
    step = pl.program_id(1)
    heads = [slice(h * HG_DK, (h + 1) * HG_DK) for h in range(HG_HEADS)]

    @pl.when(step == 0)
    def _():
        for h, cs in enumerate(heads):
            g = gm_ref[:, cs]
            ke = km_ref[:, cs] * jnp.exp(g[t - 1:t, :] - g)
            st_ref[h] = _dot_tn(vm_ref[:, cs].astype(BF16), ke.astype(BF16))

    rowid = lax.broadcasted_iota(jnp.int32, (t, HG_DK), 0)
    ones = ones_ref[...]

    def micro(i, carry):
        r0 = pl.multiple_of(i * t, t)
        for h, cs in enumerate(heads):
            q = q_ref[pl.ds(r0, t), cs]
            k = k_ref[pl.ds(r0, t), cs]
            g = g_ref[pl.ds(r0, t), cs]
            v = v_ref[pl.ds(r0, t), cs]
            st = st_ref[h]
            g_last = g[t - 1:t, :]
            o = _dot_nt((q * jnp.exp(g)).astype(BF16), st.astype(BF16))
            parts = []
            for s in range(t):
                e = jnp.where(rowid >= s, jnp.exp(jnp.minimum(g - g[s:s + 1, :], 0.0)), 0.0)
                parts.append(q * e * k[s:s + 1, :])
            dsum = _dot(jnp.concatenate(parts, axis=0).astype(BF16), ones)
            for s in range(t):
                o = o + dsum[s * t:(s + 1) * t, :] * v[s:s + 1, :]
            o_ref[pl.ds(r0, t), cs] = o
            ke = k * jnp.exp(g_last - g)
            st_ref[h] = st * jnp.exp(g_last) + _dot_tn(v.astype(BF16), ke.astype(BF16))
        return carry

    lax.fori_loop(0, tb // t, micro, 0)

    @pl.when(step == pl.num_programs(1) - 1)
    def _():
        for h in range(HG_HEADS):
            s_ref[0, h] = st_ref[h].T


def _hgrn_prompt(q, k, g, v, km, gm, vm, batch, seq, tb):
    n = q.shape[0]
    nt = seq // tb
    ones = jnp.ones((HG_DK, HG_DV), BF16)
    rspec = pl.BlockSpec((tb, HG_W), lambda b, i: (b * nt + i, 0))
    mspec = pl.BlockSpec((N_META, HG_W), lambda b, i: (0, 0))
    return pl.pallas_call(
        functools.partial(_hgrn_prompt_body, tb=tb),
        grid=(batch, nt),
        in_specs=[mspec, mspec, mspec, rspec, rspec, rspec, rspec,
                  pl.BlockSpec((HG_DK, HG_DV), lambda b, i: (0, 0))],
        out_specs=[rspec, pl.BlockSpec((1, HG_HEADS, HG_DK, HG_DV), lambda b, i: (b, 0, 0, 0))],
        out_shape=[jax.ShapeDtypeStruct((n, HG_W), F32),
                   jax.ShapeDtypeStruct((batch, HG_HEADS, HG_DK, HG_DV), F32)],
        scratch_shapes=[pltpu.VMEM((HG_HEADS, HG_DV, HG_DK), F32)],
        compiler_params=_cparams(("arbitrary", "arbitrary")),
        name="hgrn_prompt",
    )(km, gm, vm, q, k, g, v, ones)


def _hgrn_step_body(q_ref, k_ref, lf_ref, v_ref, s_ref, o_ref, sn_ref):
    for h in range(HG_HEADS):
        sn = jnp.exp(lf_ref[0, h]) * s_ref[0, h] + k_ref[0, h] * v_ref[0, h]
        sn_ref[0, h] = sn
        o_ref[0, h] = jnp.sum(q_ref[0, h] * sn, axis=0, keepdims=True)


def _hgrn_step(q, k, log_f, v, state):
    db = q.shape[0]
    col = lambda a: a.reshape(db, HG_HEADS, HG_DK, 1)
    cspec = pl.BlockSpec((1, HG_HEADS, HG_DK, 1), lambda b: (b, 0, 0, 0))
    vspec = pl.BlockSpec((1, HG_HEADS, 1, HG_DV), lambda b: (b, 0, 0, 0))
    sspec = pl.BlockSpec((1, HG_HEADS, HG_DK, HG_DV), lambda b: (b, 0, 0, 0))
    o, sn = pl.pallas_call(
        _hgrn_step_body,
        grid=(db,),
        in_specs=[cspec, cspec, cspec, vspec, sspec],
        out_specs=[vspec, sspec],
        out_shape=[jax.ShapeDtypeStruct((db, HG_HEADS, 1, HG_DV), F32),
                   jax.ShapeDtypeStruct(state.shape, F32)],
        compiler_params=_cparams(("arbitrary",)),
        name="hgrn_step",
    )(col(q), col(k), col(log_f), v.reshape(db, HG_HEADS, 1, HG_DV), state)
    return o.reshape(db, HG_W), sn


def _mix_ln_body(osb_ref, ohg_ref, gh_ref, xn_ref, sbg_ref, hgg_ref, w_ref, g_ref, b_ref, h_ref):
    osb = osb_ref[...]
    sb = osb * lax.rsqrt(jnp.mean(osb * osb, -1, keepdims=True) + EPS) * sbg_ref[...]
    delta = _dot(sb.astype(BF16), w_ref[0:SB_WIDTH, :])
    for h in range(HG_HEADS):
        cs = slice(h * HG_DV, (h + 1) * HG_DV)
        o = ohg_ref[:, cs]
        gate = gh_ref[:, cs]
        hg = o * lax.rsqrt(jnp.mean(o * o, -1, keepdims=True) + EPS) * hgg_ref[:, cs]
        hg = hg * (gate * _sigmoid(gate))
        delta = delta + _dot(hg.astype(BF16), w_ref[SB_WIDTH + h * HG_DV:SB_WIDTH + (h + 1) * HG_DV, :])
    h_ref[...] = _layer_norm(DN_ALPHA * xn_ref[...] + delta, g_ref[...], b_ref[...])


def _mix_ln(osb, ohg, gh, xn, sbg, hgg, w_bf, g, b, tm):
    n, d = xn.shape
    row = lambda w: pl.BlockSpec((tm, w), lambda i: (i, 0))
    full = lambda a: pl.BlockSpec(a.shape, lambda i: (0,) * a.ndim)
    return pl.pallas_call(
        _mix_ln_body,
        grid=(n // tm,),
        in_specs=[row(SB_WIDTH), row(HG_W), row(HG_W), row(d), full(sbg), full(hgg), full(w_bf),
                  full(g), full(b)],
        out_specs=row(d),
        out_shape=jax.ShapeDtypeStruct((n, d), F32),
        compiler_params=_cparams(("arbitrary",)),
        name="mix_ln",
    )(osb, ohg, gh, xn, sbg, hgg, w_bf, g, b)


def _top_rows(x, rid, n_rows, payload=None):
    vals, ids = [], []
    for _ in range(PEER_TOPK):
        m = jnp.max(x, axis=0, keepdims=True)
        idx = jnp.min(jnp.where(x == m, rid, float(n_rows)), axis=0, keepdims=True)
        hit = rid == idx
        vals.append(m)
        if payload is None:
            ids.append(idx)
        else:
            ids.append(jnp.max(jnp.where(hit, payload, -1.0), axis=0, keepdims=True))
        x = jnp.where(hit, -jnp.inf, x)
    return vals, ids


def _peer_route_body(h_ref, wpq_ref, keys_ref, ids_ref, gt_ref, idt_ref, *, tt):
    hb = h_ref[...].astype(BF16)
    rid_k = lax.broadcasted_iota(jnp.int32, (PEER_N_KEYS, tt), 0).astype(F32)
    rid_c = lax.broadcasted_iota(jnp.int32, (PEER_TOPK * PEER_TOPK, tt), 0).astype(F32)

    def head(hd, carry):
        tops = []
        for p in range(2):
            c0 = pl.multiple_of((hd * 2 + p) * PEER_HALF, PEER_HALF)
            q = _dot(hb, wpq_ref[:, pl.ds(c0, PEER_HALF)])
            q = q - jnp.mean(q, -1, keepdims=True)
            q = q * lax.rsqrt(jnp.mean(q * q, -1, keepdims=True) + EPS)
            s = _dot_nt(keys_ref[hd * 2 + p], q.astype(BF16))
            vals, ids = _top_rows(s, rid_k, PEER_N_KEYS)
            tops.append((jnp.concatenate(vals, 0), jnp.concatenate(ids, 0)))
        (s1, i1), (s2, i2) = tops
        cand = jnp.concatenate([s1[a:a + 1] + s2 for a in range(PEER_TOPK)], 0)
        eid = jnp.concatenate([i1[a:a + 1] * float(PEER_N_KEYS) + i2 for a in range(PEER_TOPK)], 0)
        vals, ids = _top_rows(cand, rid_c, PEER_TOPK * PEER_TOPK, payload=eid)
        sc = jnp.concatenate(vals, 0)
        ex = jnp.exp(sc - vals[0])
        r0 = pl.multiple_of(hd * PEER_TOPK, PEER_TOPK)
        gt_ref[pl.ds(r0, PEER_TOPK), :] = ex / jnp.sum(ex, 0, keepdims=True)
        idt_ref[pl.ds(r0, PEER_TOPK), :] = jnp.concatenate(ids, 0)
        return carry

    lax.fori_loop(0, PEER_HEADS, head, 0)
    ids_ref[...] = idt_ref[...].T.astype(jnp.int32)


def _peer_route(h, wpq_bf, keys_bf, tt):
    n, d = h.shape
    return pl.pallas_call(
        functools.partial(_peer_route_body, tt=tt),
        grid=(n // tt,),
        in_specs=[pl.BlockSpec((tt, d), lambda i: (i, 0)),
                  pl.BlockSpec(wpq_bf.shape, lambda i: (0, 0)),
                  pl.BlockSpec(keys_bf.shape, lambda i: (0, 0, 0))],
        out_specs=[pl.BlockSpec((tt, PEER_SEL), lambda i: (i, 0)),
                   pl.BlockSpec((PEER_SEL, tt), lambda i: (0, i))],
        out_shape=[jax.ShapeDtypeStruct((n, PEER_SEL), jnp.int32),
                   jax.ShapeDtypeStruct((PEER_SEL, n), F32)],
        scratch_shapes=[pltpu.VMEM((PEER_SEL, tt), F32)],
        compiler_params=_cparams(("arbitrary",)),
        name="peer_route",
    )(h, wpq_bf, keys_bf)


PEER_TOK = 8
SLAB = 2 * SUBLANES


def _peer_expert_body(ids_hbm, tab_hbm, h_ref, gt_ref, o_ref, ids_smem, buf, gsem, isem, *, tg):
    i = pl.program_id(0)
    nsub = tg // PEER_TOK
    total = pl.num_programs(0) * nsub
    per_blk = PEER_TOK * PEER_SEL
    lane = lax.broadcasted_iota(jnp.int32, (PEER_SEL, tg), 1)

    def ids_copy(blk, slot):
        return pltpu.make_async_copy(ids_hbm.at[blk], ids_smem.at[slot], isem.at[slot])

    def issue_gathers(slot):
        def one(n, c):
            e = ids_smem[slot, n]
            dst = pl.multiple_of((slot * per_blk + n) * SLAB, SLAB)
            pltpu.make_async_copy(tab_hbm.at[e], buf.at[pl.ds(dst, SLAB)], gsem.at[slot]).start()
            return c
        lax.fori_loop(0, per_blk, one, 0)

    def wait_gathers(slot):
        base = pl.multiple_of(slot * per_blk * SLAB, SLAB)
        pltpu.make_async_copy(buf.at[pl.ds(0, per_blk * SLAB)],
                              buf.at[pl.ds(base, per_blk * SLAB)], gsem.at[slot]).wait()

    @pl.when(i == 0)
    def _():
        ids_copy(0, 0).start()
        ids_copy(0, 0).wait()
        issue_gathers(0)

        @pl.when(total > 1)
        def _():
            ids_copy(1, 1).start()

    def sub(sb, carry):
        gsb = i * nsub + sb
        slot = gsb % 2

        @pl.when(gsb + 1 < total)
        def _():
            ids_copy(gsb + 1, 1 - slot).wait()
            issue_gathers(1 - slot)

            @pl.when(gsb + 2 < total)
            def _():
                ids_copy(gsb + 2, slot).start()

        wait_gathers(slot)
        gates = gt_ref[...]
        for t in range(PEER_TOK):
            tok = sb * PEER_TOK + t
            hrow = h_ref[tok]
            base = pl.multiple_of((slot * per_blk + t * PEER_SEL) * SLAB, SLAB)
            part = jnp.zeros((PEER_SEL, LANES), F32)
            for c in range(SUBLANES):
                u_c = buf[pl.ds(base + c, PEER_SEL, stride=SLAB), :]
                part = part + u_c * hrow[c:c + 1, :]
            act = jnp.sum(part, axis=1, keepdims=True)
            gelu = 0.5 * act * (1.0 + lax.erf(act * (2.0 ** -0.5)))
            gate = jnp.sum(jnp.where(lane == tok, gates, 0.0), axis=1, keepdims=True)
            coef = gate * gelu
            rows = []
            for c in range(SUBLANES):
                v_c = buf[pl.ds(base + SUBLANES + c, PEER_SEL, stride=SLAB), :]
                rows.append(jnp.sum(coef * v_c, axis=0, keepdims=True))
            o_ref[tok] = jnp.concatenate(rows, axis=0)
        return carry

    lax.fori_loop(0, nsub, sub, 0)


def _peer_expert(h, ids, gt, table, tg):
    n, d = h.shape
    per_blk = PEER_TOK * PEER_SEL
    h3 = h.reshape(n, SUBLANES, LANES)
    out = pl.pallas_call(
        functools.partial(_peer_expert_body, tg=tg),
        grid=(n // tg,),
        in_specs=[pl.BlockSpec(memory_space=pl.ANY), pl.BlockSpec(memory_space=pl.ANY),
                  pl.BlockSpec((tg, SUBLANES, LANES), lambda i: (i, 0, 0)),
                  pl.BlockSpec((PEER_SEL, tg), lambda i: (0, i))],
        out_specs=pl.BlockSpec((tg, SUBLANES, LANES), lambda i: (i, 0, 0)),
        out_shape=jax.ShapeDtypeStruct((n, SUBLANES, LANES), F32),
        scratch_shapes=[pltpu.SMEM((2, per_blk), jnp.int32),
                        pltpu.VMEM((2 * per_blk * SLAB, LANES), F32),
                        pltpu.SemaphoreType.DMA((2,)), pltpu.SemaphoreType.DMA((2,))],
        compiler_params=_cparams(("arbitrary",)),
        name="peer_expert",
    )(ids.reshape(n // PEER_TOK, per_blk), table, h3, gt)
    return out.reshape(n, d)


def _post_ln_body(x_ref, d_ref, g_ref, b_ref, y_ref):
    y_ref[...] = _layer_norm(DN_ALPHA * x_ref[...] + d_ref[...], g_ref[...], b_ref[...])


def _post_ln(x, delta, g, b, tm):
    n, d = x.shape
    row = pl.BlockSpec((tm, d), lambda i: (i, 0))
    full = lambda a: pl.BlockSpec(a.shape, lambda i: (0,) * a.ndim)
    return pl.pallas_call(
        _post_ln_body,
        grid=(n // tm,),
        in_specs=[row, row, full(g), full(b)],
        out_specs=row,
        out_shape=jax.ShapeDtypeStruct((n, d), F32),
        compiler_params=_cparams(("arbitrary",)),
        name="post_ln",
    )(x, delta, g, b)


def _row_tile(n, target):
    t = min(n, target)
    while n % t:
        t -= SUBLANES
    return t


def kernel(x_prompt, x_sample, cache_k, cache_v, state_hgrn, page_table, meta_tokens, emb_ln_g, emb_ln_b, w_in, sb_bias, lb_param, sb_norm_g, hg_norm_g, w_out, ln1_g, ln1_b, w_pq, peer_sub_keys, peer_u, peer_v, ln2_g, ln2_b):
    batch, seq, d = x_prompt.shape
    db, dec_seq = x_sample.shape[0], x_sample.shape[1]
    assert dec_seq == 1 and w_in.shape[0] == DEPTH and lb_param.shape[0] == DEPTH + 1
    n_p = batch * seq
    row2 = lambda a: a.reshape(1, -1).astype(F32)

    w_in_bf = w_in[0].astype(BF16)
    w_out_bf = w_out[0].astype(BF16)
    wpq_bf = w_pq[0].astype(BF16)
    keys_bf = peer_sub_keys[0].reshape(PEER_HEADS * 2, PEER_N_KEYS, PEER_HALF).astype(BF16)
    n_exp = peer_u.shape[1]
    table = jnp.concatenate([peer_u[0].reshape(n_exp, SUBLANES, LANES),
                             peer_v[0].reshape(n_exp, SUBLANES, LANES)], axis=1)
    lbp = lb_param.astype(F32)
    eg, eb = row2(emb_ln_g), row2(emb_ln_b)

    tm = _row_tile(n_p, 256)
    bd_p = _block_tril(tm, jnp.arange(tm) // HG_BLOCK)
    xn_p, q_p, k_p, v_p, qh_p, kh_p, g_p, i_p, gh_p = _ln_inproj(
        x_prompt.reshape(n_p, d), eg, eb, w_in_bf, lbp, bd_p, tm)
    n_s = N_META + db
    x_small = jnp.concatenate([meta_tokens.astype(F32), x_sample.reshape(db, d)], axis=0)
    blocks_s = jnp.concatenate([jnp.zeros((N_META,), jnp.int32), 1 + jnp.arange(db, dtype=jnp.int32)])
    small = _ln_inproj(x_small, eg, eb, w_in_bf, lbp, _block_tril(n_s, blocks_s), n_s)
    k_m, v_m, kh_m, g_m, i_m = (small[j][:N_META] for j in (2, 3, 5, 6, 7))
    xn_s, q_s, k_s, v_s, qh_s, kh_s, lf_s, i_s, gh_s = (a[N_META:] for a in small)

    bias = sb_bias[0].astype(F32)
    o_sb_p = _sb_prompt(q_p, k_p, v_p, k_m, v_m, bias, batch, seq, _row_tile(seq, 256))
    o_hg_p, s_p = _hgrn_prompt(qh_p, kh_p, g_p, i_p, kh_m, g_m, i_m, batch, seq, _row_tile(seq, 512))

    o_sb_s = _sb_decode(q_s, cache_k[0], cache_v[0], page_table, bias)
    o_hg_s, s_s = _hgrn_step(qh_s, kh_s, lf_s, i_s, state_hgrn[0].astype(F32))

    def tail(osb, ohg, gh, xn, tm_rows, tt):
        h1 = _mix_ln(osb, ohg, gh, xn, row2(sb_norm_g[0]), row2(hg_norm_g[0]), w_out_bf,
                     row2(ln1_g[0]), row2(ln1_b[0]), tm_rows)
        ids, gt = _peer_route(h1, wpq_bf, keys_bf, tt)
        po = _peer_expert(h1, ids, gt, table, LANES)
        return _post_ln(h1, po, row2(ln2_g[0]), row2(ln2_b[0]), tm_rows)

    y_p = tail(o_sb_p, o_hg_p, gh_p, xn_p, tm, _row_tile(n_p, 256))
    n_pad = -(-db // LANES) * LANES
    pad = lambda a: jnp.pad(a, ((0, n_pad - db), (0, 0)))
    y_s = tail(pad(o_sb_s), pad(o_hg_s), pad(gh_s), pad(xn_s), n_pad, n_pad)[:db]

    def with_meta(meta_rows, rows):
        m = jnp.broadcast_to(meta_rows[None], (batch, N_META, SB_WIDTH))
        full = jnp.concatenate([m, rows.reshape(batch, seq, SB_WIDTH)], axis=1)
        return full.reshape(1, batch, N_META + seq, SB_HEADS, SB_HEAD_DIM)

    dt = x_prompt.dtype
    return (y_p.reshape(batch, seq, d).astype(dt),
            y_s.reshape(db, 1, d).astype(x_sample.dtype),
            with_meta(k_m, k_p), with_meta(v_m, v_p),
            s_p[None].astype(state_hgrn.dtype),
            k_s.reshape(1, db, 1, SB_HEADS, SB_HEAD_DIM),
            v_s.reshape(1, db, 1, SB_HEADS, SB_HEAD_DIM),
            s_s[None].astype(state_hgrn.dtype))
```

```python
import functools

import jax
import jax.numpy as jnp
from jax import lax
from jax.experimental import pallas as pl
from jax.experimental.pallas import tpu as pltpu

F32 = jnp.float32
BF16 = jnp.bfloat16

N_META = 16
SB_HEADS = 8
SB_HEAD_DIM = 64
SB_WIDTH = SB_HEADS * SB_HEAD_DIM
SB_SCALE = SB_HEAD_DIM ** -0.5
HG_HEADS = 4
HG_DK = 128
HG_DV = 128
HG_W = HG_HEADS * HG_DK
HG_SCALE = HG_DK ** -0.5
HG_BLOCK = 16
PEER_HEADS = 8
PEER_N_KEYS = 128
PEER_HALF = 128
PEER_TOPK = 16
PEER_SEL = PEER_HEADS * PEER_TOPK
DEPTH = 1
DN_ALPHA = (2.0 * DEPTH) ** 0.25
EPS = 1e-5

LANES = 128
SUBLANES = 8
VMEM_LIMIT = 56 * 1024 * 1024


def _cparams(sem):
    return pltpu.CompilerParams(dimension_semantics=sem, vmem_limit_bytes=VMEM_LIMIT)


def _split2(x):
    hi = x.astype(BF16)
    lo = (x - hi.astype(F32)).astype(BF16)
    return hi, lo


def _split3(x):
    hi = x.astype(BF16)
    r = x - hi.astype(F32)
    mid = r.astype(BF16)
    lo = (r - mid.astype(F32)).astype(BF16)
    return hi, mid, lo


def _dot(a, b):
    return jnp.dot(a, b, preferred_element_type=F32)


def _dot_nt(a, b):
    return lax.dot_general(a, b, (((1,), (1,)), ((), ())), preferred_element_type=F32)


def _dot_tn(a, b):
    return lax.dot_general(a, b, (((0,), (0,)), ((), ())), preferred_element_type=F32)


def _sigmoid(x):
    return 1.0 / (1.0 + jnp.exp(-x))


def _layer_norm(x, g, b):
    mu = jnp.mean(x, -1, keepdims=True)
    xc = x - mu
    var = jnp.mean(xc * xc, -1, keepdims=True)
    return xc * lax.rsqrt(var + EPS) * g + b


def _ln_inproj_body(x_ref, g_ref, b_ref, w_ref, lbp_ref, bd_ref,
                    xn_ref, q_ref, k_ref, v_ref, qh_ref, kh_ref, gc_ref, i_ref, gh_ref):
    xn = _layer_norm(x_ref[...], g_ref[...], b_ref[...])
    xn_ref[...] = xn
    xb = xn.astype(BF16)

    def proj(c0, width):
        return _dot(xb, w_ref[:, c0:c0 + width])

    q_ref[...] = proj(0, SB_WIDTH)
    k_ref[...] = proj(SB_WIDTH, SB_WIDTH)
    v_ref[...] = proj(2 * SB_WIDTH, SB_WIDTH)
    c = 3 * SB_WIDTH
    q_raw = proj(c, HG_W)
    f_raw = proj(c + HG_W, HG_W)
    i_ref[...] = proj(c + 2 * HG_W, HG_W)
    gh_ref[...] = proj(c + 3 * HG_W, HG_W)

    p = lbp_ref[...]
    e = jnp.exp(p - jnp.max(p, 0, keepdims=True))
    lb = e[0:1] / jnp.sum(e, 0, keepdims=True)
    log_f = jnp.log(lb + (1.0 - lb) * _sigmoid(f_raw))
    kh_ref[...] = (1.0 - lb) * _sigmoid(-f_raw)
    qh_ref[...] = q_raw * _sigmoid(q_raw) * HG_SCALE
    bd = bd_ref[...]
    hi, mid, lo = _split3(log_f)
    gc_ref[...] = _dot(bd, hi) + _dot(bd, mid) + _dot(bd, lo)


def _ln_inproj(x, g, b, w_bf, lbp, bd, tm):
    n, d = x.shape
    e = w_bf.shape[1]
    row = lambda w: pl.BlockSpec((tm, w), lambda i: (i, 0))
    full = lambda a: pl.BlockSpec(a.shape, lambda i: (0,) * a.ndim)
    outs = [jax.ShapeDtypeStruct((n, d), F32)] + [jax.ShapeDtypeStruct((n, SB_WIDTH), F32)] * 8
    return pl.pallas_call(
        _ln_inproj_body,
        grid=(n // tm,),
        in_specs=[row(d), full(g), full(b), full(w_bf), full(lbp), full(bd)],
        out_specs=[row(d)] + [row(SB_WIDTH)] * 8,
        out_shape=outs,
        compiler_params=_cparams(("arbitrary",)),
        name="ln_inproj",
    )(x, g, b, w_bf, lbp, bd)


def _block_tril(n, blocks):
    i = jnp.arange(n)
    same = blocks[:, None] == blocks[None, :]
    return (same & (i[None, :] <= i[:, None])).astype(BF16)


def _softplus(z):
    return jnp.maximum(z, 0.0) + jnp.log(1.0 + jnp.exp(-jnp.abs(z)))


def _sb_tile(qh, kt, vt, bias, tri, carry, acc, mask):
    z = _dot_nt(qh, kt.astype(BF16)) + bias
    sp_full = _softplus(z)
    sp = sp_full if mask is None else jnp.where(mask, sp_full, 0.0)
    hi, lo = _split2(sp)
    after_in = _dot(hi, tri) + _dot(lo, tri)
    w = jnp.exp(z - sp_full - after_in - carry)
    if mask is not None:
        w = jnp.where(mask, w, 0.0)
    acc = acc + _dot(w.astype(BF16), vt.astype(BF16))
    carry = carry + after_in[:, 0:1] + sp[:, 0:1]
    return carry, acc


def _sb_prompt_body(bias_ref, q_ref, k_ref, v_ref, km_ref, vm_ref, tri_ref, o_ref, *, tq):
    hp = pl.program_id(1)
    qi = pl.program_id(2)
    tri = tri_ref[...]
    tri_m = tri_ref[0:N_META, 0:N_META]
    row = lax.broadcasted_iota(jnp.int32, (tq, tq), 0)
    col = lax.broadcasted_iota(jnp.int32, (tq, tq), 1)
    diag_mask = col < row
    d = SB_HEAD_DIM
    heads = [slice(hh * d, (hh + 1) * d) for hh in range(LANES // d)]
    biases = [bias_ref[hp * len(heads) + hh] for hh in range(len(heads))]
    qhs = [(q_ref[:, cs] * SB_SCALE).astype(BF16) for cs in heads]

    def step(kref, vref, rows, tri_t, state, mask):
        return tuple(_sb_tile(qh, kref[rows, cs], vref[rows, cs], bias, tri_t, c, a, mask)
                     for qh, cs, bias, (c, a) in zip(qhs, heads, biases, state))

    state = tuple((jnp.zeros((tq, 1), F32), jnp.zeros((tq, d), F32)) for _ in heads)
    state = step(k_ref, v_ref, pl.ds(pl.multiple_of(qi * tq, tq), tq), tri, state, diag_mask)

    def body(it, st):
        rows = pl.ds(pl.multiple_of((qi - 1 - it) * tq, tq), tq)
        return step(k_ref, v_ref, rows, tri, st, None)

    state = lax.fori_loop(0, qi, body, state)
    state = step(km_ref, vm_ref, slice(None), tri_m, state, None)
    for cs, (_, acc) in zip(heads, state):
        o_ref[:, cs] = acc


def _sb_prompt(q, k, v, km, vm, bias, batch, seq, tq):
    n = q.shape[0]
    nq = seq // tq
    j = jnp.arange(tq)
    tri = (j[:, None] > j[None, :]).astype(BF16)
    qspec = pl.BlockSpec((tq, LANES), lambda b, h, i: (b * nq + i, h))
    kspec = pl.BlockSpec((seq, LANES), lambda b, h, i: (b, h))
    mspec = pl.BlockSpec((N_META, LANES), lambda b, h, i: (0, h))
    return pl.pallas_call(
        functools.partial(_sb_prompt_body, tq=tq),
        grid=(batch, SB_WIDTH // LANES, nq),
        in_specs=[pl.BlockSpec(memory_space=pltpu.SMEM), qspec, kspec, kspec, mspec, mspec,
                  pl.BlockSpec((tq, tq), lambda b, h, i: (0, 0))],
        out_specs=qspec,
        out_shape=jax.ShapeDtypeStruct((n, SB_WIDTH), F32),
        compiler_params=_cparams(("arbitrary", "arbitrary", "arbitrary")),
        name="sb_prompt",
    )(bias, q, k, v, km, vm, tri)


DEC_PAGES = 4


def _sb_decode_body(pt_ref, q_ref, *refs):
    kv_refs = refs[:2 * DEC_PAGES]
    bias_ref, seg_ref, segt_ref, trit_ref, o_ref, acc_ref, carry_ref = refs[2 * DEC_PAGES:]
    p = pl.program_id(1)

    @pl.when(p == 0)
    def _():
        acc_ref[...] = jnp.zeros_like(acc_ref)
        carry_ref[...] = jnp.zeros_like(carry_ref)

    qs = q_ref[0] * SB_SCALE
    trit = trit_ref[...]
    acc = acc_ref[...]
    carry = carry_ref[...]
    for r in range(DEC_PAGES):
        k_ref, v_ref = kv_refs[r], kv_refs[DEC_PAGES + r]
        z = _dot((k_ref[...] * qs).astype(BF16), seg_ref[...]) + bias_ref[...]
        sp = _softplus(z)
        hi, lo = _split2(sp)
        after_in = _dot(trit, hi) + _dot(trit, lo)
        w = jnp.exp(z - sp - after_in - carry)
        wexp = _dot(w.astype(BF16), segt_ref[...])
        acc = acc + jnp.sum(wexp * v_ref[...], axis=0, keepdims=True)
        carry = carry + after_in[0:1, :] + sp[0:1, :]
    acc_ref[...] = acc
    carry_ref[...] = carry

    @pl.when(p == pl.num_programs(1) - 1)
    def _():
        o_ref[0] = acc


def _sb_decode(q, cache_k, cache_v, page_table, bias):
    db = q.shape[0]
    n_pages = page_table.shape[1]
    n_phys, page = cache_k.shape[0], cache_k.shape[1]
    ck = cache_k.reshape(n_phys, page, SB_WIDTH)
    cv = cache_v.reshape(n_phys, page, SB_WIDTH)
    head_of = jnp.arange(SB_WIDTH) // SB_HEAD_DIM
    seg = (head_of[:, None] == jnp.arange(LANES)[None, :]).astype(BF16)
    segt = seg.T
    j = jnp.arange(page)
    trit = (j[None, :] > j[:, None]).astype(BF16)
    bias_row = jnp.zeros((1, LANES), F32).at[0, :SB_HEADS].set(bias)
    assert n_pages % DEC_PAGES == 0

    def pspec(r):
        return pl.BlockSpec((None, page, SB_WIDTH),
                            lambda b, p, pt: (pt[(b + 1) * n_pages - 1 - (p * DEC_PAGES + r)], 0, 0))

    qspec = pl.BlockSpec((1, 1, SB_WIDTH), lambda b, p, pt: (b, 0, 0))
    full = lambda a: pl.BlockSpec(a.shape, lambda b, p, pt: (0,) * a.ndim)
    pages = [pspec(r) for r in range(DEC_PAGES)]
    out = pl.pallas_call(
        _sb_decode_body,
        grid_spec=pltpu.PrefetchScalarGridSpec(
            num_scalar_prefetch=1,
            grid=(db, n_pages // DEC_PAGES),
            in_specs=[qspec] + pages + pages + [full(bias_row), full(seg), full(segt), full(trit)],
            out_specs=qspec,
            scratch_shapes=[pltpu.VMEM((1, SB_WIDTH), F32), pltpu.VMEM((1, LANES), F32)],
        ),
        out_shape=jax.ShapeDtypeStruct((db, 1, SB_WIDTH), F32),
        compiler_params=_cparams(("arbitrary", "arbitrary")),
        name="sb_decode",
    )(page_table.reshape(-1), q.reshape(db, 1, SB_WIDTH), *([ck] * DEC_PAGES), *([cv] * DEC_PAGES),
      bias_row, seg, segt, trit)
    return out.reshape(db, SB_WIDTH)


def _hgrn_prompt_body(km_ref, gm_ref, vm_ref, q_ref, k_ref, g_ref, v_ref, ones_ref,
                      o_ref, s_ref, st_ref, *, tb):
    t = HG_BLOCK
    step = pl.program_id(1)
    heads = [slice(h * HG_DK, (h + 1) * HG_DK) for h in range(HG_HEADS)]

    @pl.when(step == 0)
    def _():
        for h, cs in enumerate(heads):
            g = gm_ref[:, cs]
            ke = km_ref[:, cs] * jnp.exp(g[t - 1:t, :] - g)
            st_ref[h] = _dot_tn(vm_ref[:, cs].astype(BF16), ke.astype(BF16))

    rowid = lax.broadcasted_iota(jnp.int32, (t, HG_DK), 0)
    ones = ones_ref[...]

    def micro(i, carry):
        r0 = pl.multiple_of(i * t, t)
        for h, cs in enumerate(heads):
            q = q_ref[pl.ds(r0, t), cs]
            k = k_ref[pl.ds(r0, t), cs]
            g = g_ref[pl.ds(r0, t), cs]
            v = v_ref[pl.ds(r0, t), cs]
            st = st_ref[h]
            g_last = g[t - 1:t, :]
            o = _dot_nt((q * jnp.exp(g)).astype(BF16), st.astype(BF16))
            parts = []
            for s in range(t):
                e = jnp.where(rowid >= s, jnp.exp(jnp.minimum(g - g[s:s + 1, :], 0.0)), 0.0)
                parts.append(q * e * k[s:s + 1, :])
            dsum = _dot(jnp.concatenate(parts, axis=0).astype(BF16), ones)
            for s in range(t):
                o = o + dsum[s * t:(s + 1) * t, :] * v[s:s + 1, :]
            o_ref[pl.ds(r0, t), cs] = o
            ke = k * jnp.exp(g_last - g)
            st_ref[h] = st * jnp.exp(g_last) + _dot_tn(v.astype(BF16), ke.astype(BF16))
        return carry

    lax.fori_loop(0, tb // t, micro, 0)

    @pl.when(step == pl.num_programs(1) - 1)
    def _():
        for h in range(HG_HEADS):
            s_ref[0, h] = st_ref[h].T


def _hgrn_prompt(q, k, g, v, km, gm, vm, batch, seq, tb):
    n = q.shape[0]
    nt = seq // tb
    ones = jnp.ones((HG_DK, HG_DV), BF16)
    rspec = pl.BlockSpec((tb, HG_W), lambda b, i: (b * nt + i, 0))
    mspec = pl.BlockSpec((N_META, HG_W), lambda b, i: (0, 0))
    return pl.pallas_call(
        functools.partial(_hgrn_prompt_body, tb=tb),
        grid=(batch, nt),
        in_specs=[mspec, mspec, mspec, rspec, rspec, rspec, rspec,
                  pl.BlockSpec((HG_DK, HG_DV), lambda b, i: (0, 0))],
        out_specs=[rspec, pl.BlockSpec((1, HG_HEADS, HG_DK, HG_DV), lambda b, i: (b, 0, 0, 0))],
        out_shape=[jax.ShapeDtypeStruct((n, HG_W), F32),
                   jax.ShapeDtypeStruct((batch, HG_HEADS, HG_DK, HG_DV), F32)],
        scratch_shapes=[pltpu.VMEM((HG_HEADS, HG_DV, HG_DK), F32)],
        compiler_params=_cparams(("arbitrary", "arbitrary")),
        name="hgrn_prompt",
    )(km, gm, vm, q, k, g, v, ones)


def _hgrn_step_body(q_ref, k_ref, lf_ref, v_ref, s_ref, o_ref, sn_ref):
    for h in range(HG_HEADS):
        sn = jnp.exp(lf_ref[0, h]) * s_ref[0, h] + k_ref[0, h] * v_ref[0, h]
        sn_ref[0, h] = sn
        o_ref[0, h] = jnp.sum(q_ref[0, h] * sn, axis=0, keepdims=True)


def _hgrn_step(q, k, log_f, v, state):
    db = q.shape[0]
    col = lambda a: a.reshape(db, HG_HEADS, HG_DK, 1)
    cspec = pl.BlockSpec((1, HG_HEADS, HG_DK, 1), lambda b: (b, 0, 0, 0))
    vspec = pl.BlockSpec((1, HG_HEADS, 1, HG_DV), lambda b: (b, 0, 0, 0))
    sspec = pl.BlockSpec((1, HG_HEADS, HG_DK, HG_DV), lambda b: (b, 0, 0, 0))
    o, sn = pl.pallas_call(
        _hgrn_step_body,
        grid=(db,),
        in_specs=[cspec, cspec, cspec, vspec, sspec],
        out_specs=[vspec, sspec],
        out_shape=[jax.ShapeDtypeStruct((db, HG_HEADS, 1, HG_DV), F32),
                   jax.ShapeDtypeStruct(state.shape, F32)],
        compiler_params=_cparams(("arbitrary",)),
        name="hgrn_step",
    )(col(q), col(k), col(log_f), v.reshape(db, HG_HEADS, 1, HG_DV), state)
    return o.reshape(db, HG_W), sn


def _mix_ln_body(osb_ref, ohg_ref, gh_ref, xn_ref, sbg_ref, hgg_ref, w_ref, g_ref, b_ref, h_ref):
    osb = osb_ref[...]
    sb = osb * lax.rsqrt(jnp.mean(osb * osb, -1, keepdims=True) + EPS) * sbg_ref[...]
    delta = _dot(sb.astype(BF16), w_ref[0:SB_WIDTH, :])
    for h in range(HG_HEADS):
        cs = slice(h * HG_DV, (h + 1) * HG_DV)
        o = ohg_ref[:, cs]
        gate = gh_ref[:, cs]
        hg = o * lax.rsqrt(jnp.mean(o * o, -1, keepdims=True) + EPS) * hgg_ref[:, cs]
        hg = hg * (gate * _sigmoid(gate))
        delta = delta + _dot(hg.astype(BF16), w_ref[SB_WIDTH + h * HG_DV:SB_WIDTH + (h + 1) * HG_DV, :])
    h_ref[...] = _layer_norm(DN_ALPHA * xn_ref[...] + delta, g_ref[...], b_ref[...])


def _mix_ln(osb, ohg, gh, xn, sbg, hgg, w_bf, g, b, tm):
    n, d = xn.shape
    row = lambda w: pl.BlockSpec((tm, w), lambda i: (i, 0))
    full = lambda a: pl.BlockSpec(a.shape, lambda i: (0,) * a.ndim)
    return pl.pallas_call(
        _mix_ln_body,
        grid=(n // tm,),
        in_specs=[row(SB_WIDTH), row(HG_W), row(HG_W), row(d), full(sbg), full(hgg), full(w_bf),
                  full(g), full(b)],
        out_specs=row(d),
        out_shape=jax.ShapeDtypeStruct((n, d), F32),
        compiler_params=_cparams(("arbitrary",)),
        name="mix_ln",
    )(osb, ohg, gh, xn, sbg, hgg, w_bf, g, b)


def _top_rows(x, rid, n_rows, payload=None):
    vals, ids = [], []
    for _ in range(PEER_TOPK):
        m = jnp.max(x, axis=0, keepdims=True)
        idx = jnp.min(jnp.where(x == m, rid, float(n_rows)), axis=0, keepdims=True)
        hit = rid == idx
        vals.append(m)
        if payload is None:
            ids.append(idx)
        else:
            ids.append(jnp.max(jnp.where(hit, payload, -1.0), axis=0, keepdims=True))
        x = jnp.where(hit, -jnp.inf, x)
    return vals, ids


def _peer_route_body(h_ref, wpq_ref, keys_ref, ids_ref, gt_ref, idt_ref, *, tt):
    hb = h_ref[...].astype(BF16)
    rid_k = lax.broadcasted_iota(jnp.int32, (PEER_N_KEYS, tt), 0).astype(F32)

    k = PEER_TOPK
    strips = [(slice(0, 1), slice(0, k)), (slice(1, 2), slice(0, 8)), (slice(2, 3), slice(0, 8)),
              (slice(3, 4), slice(0, 8)), (slice(0, k), slice(0, 1)), (slice(0, 8), slice(1, 2)),
              (slice(0, 8), slice(2, 3))]

    def grid_rows(x1, x2, scale):
        return jnp.concatenate([x1[sa] * scale + x2[sb] for sa, sb in strips], 0)

    def strip_iota(sa, sb):
        n = max(sa.stop - sa.start, sb.stop - sb.start)
        r = lax.broadcasted_iota(jnp.int32, (n, tt), 0).astype(F32)
        a = r + float(sa.start) if sa.stop - sa.start > 1 else jnp.full((n, tt), float(sa.start), F32)
        b = r + float(sb.start) if sb.stop - sb.start > 1 else jnp.full((n, tt), float(sb.start), F32)
        return a, b

    ab = [strip_iota(sa, sb) for sa, sb in strips]
    rid_c = jnp.concatenate([a * float(k) + b for a, b in ab], 0)
    dup = jnp.concatenate([jnp.where(a < 4.0, -jnp.inf, 0.0) if i >= 4 else jnp.zeros_like(a)
                           for i, (a, b) in enumerate(ab)], 0)

    def head(hd, carry):
        tops = []
        for p in range(2):
            c0 = pl.multiple_of((hd * 2 + p) * PEER_HALF, PEER_HALF)
            q = _dot(hb, wpq_ref[:, pl.ds(c0, PEER_HALF)])
            q = q - jnp.mean(q, -1, keepdims=True)
            q = q * lax.rsqrt(jnp.mean(q * q, -1, keepdims=True) + EPS)
            s = _dot_nt(keys_ref[hd * 2 + p], q.astype(BF16))
            vals, ids = _top_rows(s, rid_k, PEER_N_KEYS)
            tops.append((jnp.concatenate(vals, 0), jnp.concatenate(ids, 0)))
        (s1, i1), (s2, i2) = tops
        cand = grid_rows(s1, s2, 1.0) + dup
        eid = grid_rows(i1, i2, float(PEER_N_KEYS))
        vals, ids = _top_rows(cand, rid_c, PEER_TOPK * PEER_TOPK, payload=eid)
        sc = jnp.concatenate(vals, 0)
        ex = jnp.exp(sc - vals[0])
        r0 = pl.multiple_of(hd * PEER_TOPK, PEER_TOPK)
        gt_ref[pl.ds(r0, PEER_TOPK), :] = ex / jnp.sum(ex, 0, keepdims=True)
        idt_ref[pl.ds(r0, PEER_TOPK), :] = jnp.concatenate(ids, 0)
        return carry

    lax.fori_loop(0, PEER_HEADS, head, 0)
    ids_ref[...] = idt_ref[...].T.astype(jnp.int32)


def _peer_route(h, wpq_bf, keys_bf, tt):
    n, d = h.shape
    return pl.pallas_call(
        functools.partial(_peer_route_body, tt=tt),
        grid=(n // tt,),
        in_specs=[pl.BlockSpec((tt, d), lambda i: (i, 0)),
                  pl.BlockSpec(wpq_bf.shape, lambda i: (0, 0)),
                  pl.BlockSpec(keys_bf.shape, lambda i: (0, 0, 0))],
        out_specs=[pl.BlockSpec((tt, PEER_SEL), lambda i: (i, 0)),
                   pl.BlockSpec((PEER_SEL, tt), lambda i: (0, i))],
        out_shape=[jax.ShapeDtypeStruct((n, PEER_SEL), jnp.int32),
                   jax.ShapeDtypeStruct((PEER_SEL, n), F32)],
        scratch_shapes=[pltpu.VMEM((PEER_SEL, tt), F32)],
        compiler_params=_cparams(("arbitrary",)),
        name="peer_route",
    )(h, wpq_bf, keys_bf)


PEER_TOK = 8
SLAB = SUBLANES
PEER_RING = 3


def _peer_table(u, v):
    n_exp = u.shape[0]
    x = jnp.concatenate([u.reshape(n_exp, SUBLANES, LANES), v.reshape(n_exp, SUBLANES, LANES)], axis=1)
    bits = lax.bitcast_convert_type(x.astype(BF16), jnp.uint16).astype(jnp.uint32)
    words = bits[:, 0::2, :] | (bits[:, 1::2, :] << 16)
    return lax.bitcast_convert_type(words, jnp.int32)


def _unpack_pair(w):
    lo = pltpu.bitcast(w << 16, F32)
    hi = pltpu.bitcast(w & jnp.int32(-65536), F32)
    return lo, hi


def _peer_expert_body(ids_hbm, tab_hbm, h_ref, gt_ref, o_ref, ids_smem, buf, gsem, isem, *, tg):
    i = pl.program_id(0)
    nsub = tg // PEER_TOK
    total = pl.num_programs(0) * nsub
    last = total - 1
    per_blk = PEER_TOK * PEER_SEL
    lane = lax.broadcasted_iota(jnp.int32, (PEER_SEL, tg), 1)

    def ids_copy(blk, slot):
        dst = ids_smem.at[pl.ds(pl.multiple_of(slot * per_blk, per_blk), per_blk)]
        return pltpu.make_async_copy(ids_hbm.at[jnp.minimum(blk, last)], dst, isem.at[slot])

    def gather(slot, n, priority):
        e = ids_smem[slot * per_blk + n]
        dst = pl.multiple_of((slot * per_blk + n) * SLAB, SLAB)
        pltpu.make_async_copy(tab_hbm.at[e], buf.at[pl.ds(dst, SLAB)], gsem.at[slot]).start(priority=priority)

    def wait_gathers(slot):
        base = pl.multiple_of(slot * per_blk * SLAB, SLAB)
        pltpu.make_async_copy(buf.at[pl.ds(0, per_blk * SLAB)],
                              buf.at[pl.ds(base, per_blk * SLAB)], gsem.at[slot]).wait()

    @pl.when(i == 0)
    def _():
        for b in range(PEER_RING):
            ids_copy(b, b).start()
        for b in range(PEER_RING - 1):
            ids_copy(b, b).wait()

            def one(n, c):
                gather(b, n, 0)
                return c
            lax.fori_loop(0, per_blk, one, 0)

    def sub(sb, carry):
        g = i * nsub + sb
        slot = g % PEER_RING
        nslot = (g + PEER_RING - 1) % PEER_RING
        ids_copy(g + PEER_RING - 1, nslot).wait()
        wait_gathers(slot)
        gates = gt_ref[...]
        for t in range(PEER_TOK):
            for j in range(PEER_SEL):
                gather(nslot, t * PEER_SEL + j, j % 2)
            tok = sb * PEER_TOK + t
            hrow = h_ref[tok]
            base = pl.multiple_of((slot * per_blk + t * PEER_SEL) * SLAB, SLAB)
            part = jnp.zeros((PEER_SEL, LANES), F32)
            for s in range(SUBLANES // 2):
                lo, hi = _unpack_pair(buf[pl.ds(base + s, PEER_SEL, stride=SLAB), :])
                part = part + lo * hrow[2 * s:2 * s + 1, :] + hi * hrow[2 * s + 1:2 * s + 2, :]
            act = jnp.sum(part, axis=1, keepdims=True)
            gelu = 0.5 * act * (1.0 + lax.erf(act * (2.0 ** -0.5)))
            gate = jnp.sum(jnp.where(lane == tok, gates, 0.0), axis=1, keepdims=True)
            coef = gate * gelu
            rows = []
            for s in range(SUBLANES // 2, SUBLANES):
                lo, hi = _unpack_pair(buf[pl.ds(base + s, PEER_SEL, stride=SLAB), :])
                rows.append(jnp.sum(coef * lo, axis=0, keepdims=True))
                rows.append(jnp.sum(coef * hi, axis=0, keepdims=True))
            o_ref[tok] = jnp.concatenate(rows, axis=0)
        ids_copy(g + PEER_RING, slot).start()
        return carry

    lax.fori_loop(0, nsub, sub, 0)

    @pl.when(i == pl.num_programs(0) - 1)
    def _():
        for k in range(1, PEER_RING):
            wait_gathers((last + k) % PEER_RING)
        ids_copy(last + PEER_RING, last % PEER_RING).wait()


def _peer_expert(h, ids, gt, table, tg):
    n, d = h.shape
    per_blk = PEER_TOK * PEER_SEL
    h3 = h.reshape(n, SUBLANES, LANES)
    out = pl.pallas_call(
        functools.partial(_peer_expert_body, tg=tg),
        grid=(n // tg,),
        in_specs=[pl.BlockSpec(memory_space=pl.ANY), pl.BlockSpec(memory_space=pl.ANY),
                  pl.BlockSpec((tg, SUBLANES, LANES), lambda i: (i, 0, 0)),
                  pl.BlockSpec((PEER_SEL, tg), lambda i: (0, i))],
        out_specs=pl.BlockSpec((tg, SUBLANES, LANES), lambda i: (i, 0, 0)),
        out_shape=jax.ShapeDtypeStruct((n, SUBLANES, LANES), F32),
        scratch_shapes=[pltpu.SMEM((PEER_RING * per_blk,), jnp.int32),
                        pltpu.VMEM((PEER_RING * per_blk * SLAB, LANES), jnp.int32),
                        pltpu.SemaphoreType.DMA((PEER_RING,)), pltpu.SemaphoreType.DMA((PEER_RING,))],
        compiler_params=_cparams(("arbitrary",)),
        name="peer_expert",
    )(ids.reshape(n // PEER_TOK, per_blk), table, h3, gt)
    return out.reshape(n, d)


def _post_ln_body(x_ref, d_ref, g_ref, b_ref, y_ref):
    y_ref[...] = _layer_norm(DN_ALPHA * x_ref[...] + d_ref[...], g_ref[...], b_ref[...])


def _post_ln(x, delta, g, b, tm):
    n, d = x.shape
    row = pl.BlockSpec((tm, d), lambda i: (i, 0))
    full = lambda a: pl.BlockSpec(a.shape, lambda i: (0,) * a.ndim)
    return pl.pallas_call(
        _post_ln_body,
        grid=(n // tm,),
        in_specs=[row, row, full(g), full(b)],
        out_specs=row,
        out_shape=jax.ShapeDtypeStruct((n, d), F32),
        compiler_params=_cparams(("arbitrary",)),
        name="post_ln",
    )(x, delta, g, b)


def _row_tile(n, target):
    t = min(n, target)
    while n % t:
        t -= SUBLANES
    return t


def kernel(x_prompt, x_sample, cache_k, cache_v, state_hgrn, page_table, meta_tokens, emb_ln_g, emb_ln_b, w_in, sb_bias, lb_param, sb_norm_g, hg_norm_g, w_out, ln1_g, ln1_b, w_pq, peer_sub_keys, peer_u, peer_v, ln2_g, ln2_b):
    batch, seq, d = x_prompt.shape
    db, dec_seq = x_sample.shape[0], x_sample.shape[1]
    assert dec_seq == 1 and w_in.shape[0] == DEPTH and lb_param.shape[0] == DEPTH + 1
    n_p = batch * seq
    row2 = lambda a: a.reshape(1, -1).astype(F32)

    w_in_bf = w_in[0].astype(BF16)
    w_out_bf = w_out[0].astype(BF16)
    wpq_bf = w_pq[0].astype(BF16)
    keys_bf = peer_sub_keys[0].reshape(PEER_HEADS * 2, PEER_N_KEYS, PEER_HALF).astype(BF16)
    table = _peer_table(peer_u[0], peer_v[0])
    lbp = lb_param.astype(F32)
    eg, eb = row2(emb_ln_g), row2(emb_ln_b)

    tm = _row_tile(n_p, 256)
    bd_p = _block_tril(tm, jnp.arange(tm) // HG_BLOCK)
    xn_p, q_p, k_p, v_p, qh_p, kh_p, g_p, i_p, gh_p = _ln_inproj(
        x_prompt.reshape(n_p, d), eg, eb, w_in_bf, lbp, bd_p, tm)
    n_s = N_META + db
    x_small = jnp.concatenate([meta_tokens.astype(F32), x_sample.reshape(db, d)], axis=0)
    blocks_s = jnp.concatenate([jnp.zeros((N_META,), jnp.int32), 1 + jnp.arange(db, dtype=jnp.int32)])
    small = _ln_inproj(x_small, eg, eb, w_in_bf, lbp, _block_tril(n_s, blocks_s), n_s)
    k_m, v_m, kh_m, g_m, i_m = (small[j][:N_META] for j in (2, 3, 5, 6, 7))
    xn_s, q_s, k_s, v_s, qh_s, kh_s, lf_s, i_s, gh_s = (a[N_META:] for a in small)

    bias = sb_bias[0].astype(F32)
    o_sb_p = _sb_prompt(q_p, k_p, v_p, k_m, v_m, bias, batch, seq, _row_tile(seq, 256))
    o_hg_p, s_p = _hgrn_prompt(qh_p, kh_p, g_p, i_p, kh_m, g_m, i_m, batch, seq, _row_tile(seq, 512))

    o_sb_s = _sb_decode(q_s, cache_k[0], cache_v[0], page_table, bias)
    o_hg_s, s_s = _hgrn_step(qh_s, kh_s, lf_s, i_s, state_hgrn[0].astype(F32))

    def tail(osb, ohg, gh, xn, tm_rows, tt):
        h1 = _mix_ln(osb, ohg, gh, xn, row2(sb_norm_g[0]), row2(hg_norm_g[0]), w_out_bf,
                     row2(ln1_g[0]), row2(ln1_b[0]), tm_rows)
        ids, gt = _peer_route(h1, wpq_bf, keys_bf, tt)
        po = _peer_expert(h1, ids, gt, table, LANES)
        return _post_ln(h1, po, row2(ln2_g[0]), row2(ln2_b[0]), tm_rows)

    y_p = tail(o_sb_p, o_hg_p, gh_p, xn_p, tm, _row_tile(n_p, 256))
    n_pad = -(-db // LANES) * LANES
    pad = lambda a: jnp.pad(a, ((0, n_pad - db), (0, 0)))
    y_s = tail(pad(o_sb_s), pad(o_hg_s), pad(gh_s), pad(xn_s), n_pad, n_pad)[:db]

    def with_meta(meta_rows, rows):
        m = jnp.broadcast_to(meta_rows[None], (batch, N_META, SB_WIDTH))
        full = jnp.concatenate([m, rows.reshape(batch, seq, SB_WIDTH)], axis=1)
        return full.reshape(1, batch, N_META + seq, SB_HEADS, SB_HEAD_DIM)

    dt = x_prompt.dtype
    return (y_p.reshape(batch, seq, d).astype(dt),
            y_s.reshape(db, 1, d).astype(x_sample.dtype),
            with_meta(k_m, k_p), with_meta(v_m, v_p),
            s_p[None].astype(state_hgrn.dtype),
            k_s.reshape(1, db, 1, SB_HEADS, SB_HEAD_DIM),
            v_s.reshape(1, db, 1, SB_HEADS, SB_HEAD_DIM),
            s_s[None].astype(state_hgrn.dtype))
```

```python
import functools

import jax
import jax.numpy as jnp
from jax import lax
from jax.experimental import pallas as pl
from jax.experimental.pallas import tpu as pltpu

F32 = jnp.float32
BF16 = jnp.bfloat16

N_META = 16
SB_HEADS = 8
SB_HEAD_DIM = 64
SB_WIDTH = SB_HEADS * SB_HEAD_DIM
SB_SCALE = SB_HEAD_DIM ** -0.5
HG_HEADS = 4
HG_DK = 128
HG_DV = 128
HG_W = HG_HEADS * HG_DK
HG_SCALE = HG_DK ** -0.5
HG_BLOCK = 16
PEER_HEADS = 8
PEER_N_KEYS = 128
PEER_HALF = 128
PEER_TOPK = 16
PEER_SEL = PEER_HEADS * PEER_TOPK
DEPTH = 1
DN_ALPHA = (2.0 * DEPTH) ** 0.25
EPS = 1e-5

LANES = 128
SUBLANES = 8
VMEM_LIMIT = 56 * 1024 * 1024


def _cparams(sem):
    return pltpu.CompilerParams(dimension_semantics=sem, vmem_limit_bytes=VMEM_LIMIT)


def _split2(x):
    hi = x.astype(BF16)
    lo = (x - hi.astype(F32)).astype(BF16)
    return hi, lo


def _split3(x):
    hi = x.astype(BF16)
    r = x - hi.astype(F32)
    mid = r.astype(BF16)
    lo = (r - mid.astype(F32)).astype(BF16)
    return hi, mid, lo


def _dot(a, b):
    return jnp.dot(a, b, preferred_element_type=F32)


def _dot_nt(a, b):
    return lax.dot_general(a, b, (((1,), (1,)), ((), ())), preferred_element_type=F32)


def _dot_tn(a, b):
    return lax.dot_general(a, b, (((0,), (0,)), ((), ())), preferred_element_type=F32)


def _sigmoid(x):
    return 1.0 / (1.0 + jnp.exp(-x))


def _layer_norm(x, g, b):
    mu = jnp.mean(x, -1, keepdims=True)
    xc = x - mu
    var = jnp.mean(xc * xc, -1, keepdims=True)
    return xc * lax.rsqrt(var + EPS) * g + b


def _ln_inproj_body(x_ref, g_ref, b_ref, w_ref, lbp_ref, bd_ref,
                    xn_ref, q_ref, k_ref, v_ref, qh_ref, kh_ref, gc_ref, i_ref, gh_ref):
    xn = _layer_norm(x_ref[...], g_ref[...], b_ref[...])
    xn_ref[...] = xn
    xb = xn.astype(BF16)

    def proj(c0, width):
        return _dot(xb, w_ref[:, c0:c0 + width])

    q_ref[...] = proj(0, SB_WIDTH)
    k_ref[...] = proj(SB_WIDTH, SB_WIDTH)
    v_ref[...] = proj(2 * SB_WIDTH, SB_WIDTH)
    c = 3 * SB_WIDTH
    q_raw = proj(c, HG_W)
    f_raw = proj(c + HG_W, HG_W)
    i_ref[...] = proj(c + 2 * HG_W, HG_W)
    gh_ref[...] = proj(c + 3 * HG_W, HG_W)

    p = lbp_ref[...]
    e = jnp.exp(p - jnp.max(p, 0, keepdims=True))
    lb = e[0:1] / jnp.sum(e, 0, keepdims=True)
    log_f = jnp.log(lb + (1.0 - lb) * _sigmoid(f_raw))
    kh_ref[...] = (1.0 - lb) * _sigmoid(-f_raw)
    qh_ref[...] = q_raw * _sigmoid(q_raw) * HG_SCALE
    bd = bd_ref[...]
    hi, mid, lo = _split3(log_f)
    gc_ref[...] = _dot(bd, hi) + _dot(bd, mid) + _dot(bd, lo)


def _ln_inproj(x, g, b, w_bf, lbp, bd, tm):
    n, d = x.shape
    e = w_bf.shape[1]
    row = lambda w: pl.BlockSpec((tm, w), lambda i: (i, 0))
    full = lambda a: pl.BlockSpec(a.shape, lambda i: (0,) * a.ndim)
    outs = [jax.ShapeDtypeStruct((n, d), F32)] + [jax.ShapeDtypeStruct((n, SB_WIDTH), F32)] * 8
    return pl.pallas_call(
        _ln_inproj_body,
        grid=(n // tm,),
        in_specs=[row(d), full(g), full(b), full(w_bf), full(lbp), full(bd)],
        out_specs=[row(d)] + [row(SB_WIDTH)] * 8,
        out_shape=outs,
        compiler_params=_cparams(("arbitrary",)),
        name="ln_inproj",
    )(x, g, b, w_bf, lbp, bd)


def _block_tril(n, blocks):
    i = jnp.arange(n)
    same = blocks[:, None] == blocks[None, :]
    return (same & (i[None, :] <= i[:, None])).astype(BF16)


SB_TILE_W = 256


def _softplus(z):
    return jnp.maximum(z, 0.0) + jnp.log(1.0 + jnp.exp(-jnp.abs(z)))


def _sb_tile(qh, kt, vt, bias, tri, carry, acc, mask):
    z = _dot_nt(qh, kt.astype(BF16)) + bias
    sp_full = _softplus(z)
    sp = sp_full if mask is None else jnp.where(mask, sp_full, 0.0)
    hi, lo = _split2(sp)
    after_in = _dot(hi, tri) + _dot(lo, tri)
    w = jnp.exp(z - sp_full - after_in - carry)
    if mask is not None:
        w = jnp.where(mask, w, 0.0)
    acc = acc + _dot(w.astype(BF16), vt.astype(BF16))
    carry = carry + after_in[:, 0:1] + sp[:, 0:1]
    return carry, acc


def _sb_prompt_body(bias_ref, q_ref, k_ref, v_ref, km_ref, vm_ref, tri_ref, o_ref, *, tq):
    hp = pl.program_id(1)
    qi = pl.program_id(2)
    tri = tri_ref[...]
    tri_m = tri_ref[0:N_META, 0:N_META]
    row = lax.broadcasted_iota(jnp.int32, (tq, tq), 0)
    col = lax.broadcasted_iota(jnp.int32, (tq, tq), 1)
    diag_mask = col < row
    d = SB_HEAD_DIM
    heads = [slice(hh * d, (hh + 1) * d) for hh in range(SB_TILE_W // d)]
    biases = [bias_ref[hp * len(heads) + hh] for hh in range(len(heads))]
    qhs = [(q_ref[:, cs] * SB_SCALE).astype(BF16) for cs in heads]

    def step(kref, vref, rows, tri_t, state, mask):
        return tuple(_sb_tile(qh, kref[rows, cs], vref[rows, cs], bias, tri_t, c, a, mask)
                     for qh, cs, bias, (c, a) in zip(qhs, heads, biases, state))

    state = tuple((jnp.zeros((tq, 1), F32), jnp.zeros((tq, d), F32)) for _ in heads)
    state = step(k_ref, v_ref, pl.ds(pl.multiple_of(qi * tq, tq), tq), tri, state, diag_mask)

    def body(it, st):
        rows = pl.ds(pl.multiple_of((qi - 1 - it) * tq, tq), tq)
        return step(k_ref, v_ref, rows, tri, st, None)

    state = lax.fori_loop(0, qi, body, state)
    state = step(km_ref, vm_ref, slice(None), tri_m, state, None)
    for cs, (_, acc) in zip(heads, state):
        o_ref[:, cs] = acc


def _sb_prompt(q, k, v, km, vm, bias, batch, seq, tq):
    n = q.shape[0]
    nq = seq // tq
    j = jnp.arange(tq)
    tri = (j[:, None] > j[None, :]).astype(BF16)
    qspec = pl.BlockSpec((tq, SB_TILE_W), lambda b, h, i: (b * nq + i, h))
    kspec = pl.BlockSpec((seq, SB_TILE_W), lambda b, h, i: (b, h))
    mspec = pl.BlockSpec((N_META, SB_TILE_W), lambda b, h, i: (0, h))
    return pl.pallas_call(
        functools.partial(_sb_prompt_body, tq=tq),
        grid=(batch, SB_WIDTH // SB_TILE_W, nq),
        in_specs=[pl.BlockSpec(memory_space=pltpu.SMEM), qspec, kspec, kspec, mspec, mspec,
                  pl.BlockSpec((tq, tq), lambda b, h, i: (0, 0))],
        out_specs=qspec,
        out_shape=jax.ShapeDtypeStruct((n, SB_WIDTH), F32),
        compiler_params=_cparams(("arbitrary", "arbitrary", "arbitrary")),
        name="sb_prompt",
    )(bias, q, k, v, km, vm, tri)


DEC_PAGES = 4


def _sb_decode_body(pt_ref, q_ref, *refs):
    kv_refs = refs[:2 * DEC_PAGES]
    bias_ref, seg_ref, segt_ref, trit_ref, o_ref, acc_ref, carry_ref = refs[2 * DEC_PAGES:]
    p = pl.program_id(1)

    @pl.when(p == 0)
    def _():
        acc_ref[...] = jnp.zeros_like(acc_ref)
        carry_ref[...] = jnp.zeros_like(carry_ref)

    qs = q_ref[0] * SB_SCALE
    trit = trit_ref[...]
    acc = acc_ref[...]
    carry = carry_ref[...]
    for r in range(DEC_PAGES):
        k_ref, v_ref = kv_refs[r], kv_refs[DEC_PAGES + r]
        z = _dot((k_ref[...] * qs).astype(BF16), seg_ref[...]) + bias_ref[...]
        sp = _softplus(z)
        hi, lo = _split2(sp)
        after_in = _dot(trit, hi) + _dot(trit, lo)
        w = jnp.exp(z - sp - after_in - carry)
        wexp = _dot(w.astype(BF16), segt_ref[...])
        acc = acc + jnp.sum(wexp * v_ref[...], axis=0, keepdims=True)
        carry = carry + after_in[0:1, :] + sp[0:1, :]
    acc_ref[...] = acc
    carry_ref[...] = carry

    @pl.when(p == pl.num_programs(1) - 1)
    def _():
        o_ref[0] = acc


def _sb_decode(q, cache_k, cache_v, page_table, bias):
    db = q.shape[0]
    n_pages = page_table.shape[1]
    n_phys, page = cache_k.shape[0], cache_k.shape[1]
    ck = cache_k.reshape(n_phys, page, SB_WIDTH)
    cv = cache_v.reshape(n_phys, page, SB_WIDTH)
    head_of = jnp.arange(SB_WIDTH) // SB_HEAD_DIM
    seg = (head_of[:, None] == jnp.arange(LANES)[None, :]).astype(BF16)
    segt = seg.T
    j = jnp.arange(page)
    trit = (j[None, :] > j[:, None]).astype(BF16)
    bias_row = jnp.zeros((1, LANES), F32).at[0, :SB_HEADS].set(bias)
    assert n_pages % DEC_PAGES == 0

    def pspec(r):
        return pl.BlockSpec((None, page, SB_WIDTH),
                            lambda b, p, pt: (pt[(b + 1) * n_pages - 1 - (p * DEC_PAGES + r)], 0, 0))

    qspec = pl.BlockSpec((1, 1, SB_WIDTH), lambda b, p, pt: (b, 0, 0))
    full = lambda a: pl.BlockSpec(a.shape, lambda b, p, pt: (0,) * a.ndim)
    pages = [pspec(r) for r in range(DEC_PAGES)]
    out = pl.pallas_call(
        _sb_decode_body,
        grid_spec=pltpu.PrefetchScalarGridSpec(
            num_scalar_prefetch=1,
            grid=(db, n_pages // DEC_PAGES),
            in_specs=[qspec] + pages + pages + [full(bias_row), full(seg), full(segt), full(trit)],
            out_specs=qspec,
            scratch_shapes=[pltpu.VMEM((1, SB_WIDTH), F32), pltpu.VMEM((1, LANES), F32)],
        ),
        out_shape=jax.ShapeDtypeStruct((db, 1, SB_WIDTH), F32),
        compiler_params=_cparams(("arbitrary", "arbitrary")),
        name="sb_decode",
    )(page_table.reshape(-1), q.reshape(db, 1, SB_WIDTH), *([ck] * DEC_PAGES), *([cv] * DEC_PAGES),
      bias_row, seg, segt, trit)
    return out.reshape(db, SB_WIDTH)


def _hgrn_prompt_body(km_ref, gm_ref, vm_ref, q_ref, k_ref, g_ref, v_ref, ones_ref,
                      o_ref, s_ref, st_ref, *, tb):
    t = HG_BLOCK
    step = pl.program_id(1)
    heads = [slice(h * HG_DK, (h + 1) * HG_DK) for h in range(HG_HEADS)]

    @pl.when(step == 0)
    def _():
        for h, cs in enumerate(heads):
            g = gm_ref[:, cs]
            ke = km_ref[:, cs] * jnp.exp(g[t - 1:t, :] - g)
            st_ref[h] = _dot_tn(vm_ref[:, cs].astype(BF16), ke.astype(BF16))

    rowid = lax.broadcasted_iota(jnp.int32, (t, HG_DK), 0)
    ones = ones_ref[...]

    def micro(i, carry):
        r0 = pl.multiple_of(i * t, t)
        for h, cs in enumerate(heads):
            q = q_ref[pl.ds(r0, t), cs]
            k = k_ref[pl.ds(r0, t), cs]
            g = g_ref[pl.ds(r0, t), cs]
            v = v_ref[pl.ds(r0, t), cs]
            st = st_ref[h]
            g_last = g[t - 1:t, :]
            o = _dot_nt((q * jnp.exp(g)).astype(BF16), st.astype(BF16))
            parts = []
            for s in range(t):
                e = jnp.where(rowid >= s, jnp.exp(jnp.minimum(g - g[s:s + 1, :], 0.0)), 0.0)
                parts.append(q * e * k[s:s + 1, :])
            dsum = _dot(jnp.concatenate(parts, axis=0).astype(BF16), ones)
            for s in range(t):
                o = o + dsum[s * t:(s + 1) * t, :] * v[s:s + 1, :]
            o_ref[pl.ds(r0, t), cs] = o
            ke = k * jnp.exp(g_last - g)
            st_ref[h] = st * jnp.exp(g_last) + _dot_tn(v.astype(BF16), ke.astype(BF16))
        return carry

    lax.fori_loop(0, tb // t, micro, 0)

    @pl.when(step == pl.num_programs(1) - 1)
    def _():
        for h in range(HG_HEADS):
            s_ref[0, h] = st_ref[h].T


def _hgrn_prompt(q, k, g, v, km, gm, vm, batch, seq, tb):
    n = q.shape[0]
    nt = seq // tb
    ones = jnp.ones((HG_DK, HG_DV), BF16)
    rspec = pl.BlockSpec((tb, HG_W), lambda b, i: (b * nt + i, 0))
    mspec = pl.BlockSpec((N_META, HG_W), lambda b, i: (0, 0))
    return pl.pallas_call(
        functools.partial(_hgrn_prompt_body, tb=tb),
        grid=(batch, nt),
        in_specs=[mspec, mspec, mspec, rspec, rspec, rspec, rspec,
                  pl.BlockSpec((HG_DK, HG_DV), lambda b, i: (0, 0))],
        out_specs=[rspec, pl.BlockSpec((1, HG_HEADS, HG_DK, HG_DV), lambda b, i: (b, 0, 0, 0))],
        out_shape=[jax.ShapeDtypeStruct((n, HG_W), F32),
                   jax.ShapeDtypeStruct((batch, HG_HEADS, HG_DK, HG_DV), F32)],
        scratch_shapes=[pltpu.VMEM((HG_HEADS, HG_DV, HG_DK), F32)],
        compiler_params=_cparams(("arbitrary", "arbitrary")),
        name="hgrn_prompt",
    )(km, gm, vm, q, k, g, v, ones)


def _hgrn_step_body(q_ref, k_ref, lf_ref, v_ref, s_ref, o_ref, sn_ref):
    for h in range(HG_HEADS):
        sn = jnp.exp(lf_ref[0, h]) * s_ref[0, h] + k_ref[0, h] * v_ref[0, h]
        sn_ref[0, h] = sn
        o_ref[0, h] = jnp.sum(q_ref[0, h] * sn, axis=0, keepdims=True)


def _hgrn_step(q, k, log_f, v, state):
    db = q.shape[0]
    col = lambda a: a.reshape(db, HG_HEADS, HG_DK, 1)
    cspec = pl.BlockSpec((1, HG_HEADS, HG_DK, 1), lambda b: (b, 0, 0, 0))
    vspec = pl.BlockSpec((1, HG_HEADS, 1, HG_DV), lambda b: (b, 0, 0, 0))
    sspec = pl.BlockSpec((1, HG_HEADS, HG_DK, HG_DV), lambda b: (b, 0, 0, 0))
    o, sn = pl.pallas_call(
        _hgrn_step_body,
        grid=(db,),
        in_specs=[cspec, cspec, cspec, vspec, sspec],
        out_specs=[vspec, sspec],
        out_shape=[jax.ShapeDtypeStruct((db, HG_HEADS, 1, HG_DV), F32),
                   jax.ShapeDtypeStruct(state.shape, F32)],
        compiler_params=_cparams(("arbitrary",)),
        name="hgrn_step",
    )(col(q), col(k), col(log_f), v.reshape(db, HG_HEADS, 1, HG_DV), state)
    return o.reshape(db, HG_W), sn


def _mix_ln_body(osb_ref, ohg_ref, gh_ref, xn_ref, sbg_ref, hgg_ref, w_ref, g_ref, b_ref, h_ref):
    osb = osb_ref[...]
    sb = osb * lax.rsqrt(jnp.mean(osb * osb, -1, keepdims=True) + EPS) * sbg_ref[...]
    delta = _dot(sb.astype(BF16), w_ref[0:SB_WIDTH, :])
    for h in range(HG_HEADS):
        cs = slice(h * HG_DV, (h + 1) * HG_DV)
        o = ohg_ref[:, cs]
        gate = gh_ref[:, cs]
        hg = o * lax.rsqrt(jnp.mean(o * o, -1, keepdims=True) + EPS) * hgg_ref[:, cs]
        hg = hg * (gate * _sigmoid(gate))
        delta = delta + _dot(hg.astype(BF16), w_ref[SB_WIDTH + h * HG_DV:SB_WIDTH + (h + 1) * HG_DV, :])
    h_ref[...] = _layer_norm(DN_ALPHA * xn_ref[...] + delta, g_ref[...], b_ref[...])


def _mix_ln(osb, ohg, gh, xn, sbg, hgg, w_bf, g, b, tm):
    n, d = xn.shape
    row = lambda w: pl.BlockSpec((tm, w), lambda i: (i, 0))
    full = lambda a: pl.BlockSpec(a.shape, lambda i: (0,) * a.ndim)
    return pl.pallas_call(
        _mix_ln_body,
        grid=(n // tm,),
        in_specs=[row(SB_WIDTH), row(HG_W), row(HG_W), row(d), full(sbg), full(hgg), full(w_bf),
                  full(g), full(b)],
        out_specs=row(d),
        out_shape=jax.ShapeDtypeStruct((n, d), F32),
        compiler_params=_cparams(("arbitrary",)),
        name="mix_ln",
    )(osb, ohg, gh, xn, sbg, hgg, w_bf, g, b)


def _top_rows(x, rid, n_rows, payload=None):
    vals, ids = [], []
    for _ in range(PEER_TOPK):
        m = jnp.max(x, axis=0, keepdims=True)
        idx = jnp.min(jnp.where(x == m, rid, float(n_rows)), axis=0, keepdims=True)
        hit = rid == idx
        vals.append(m)
        if payload is None:
            ids.append(idx)
        else:
            ids.append(jnp.max(jnp.where(hit, payload, -1.0), axis=0, keepdims=True))
        x = jnp.where(hit, -jnp.inf, x)
    return vals, ids


def _peer_route_body(h_ref, wpq_ref, keys_ref, ids_ref, gt_ref, idt_ref, *, tt):
    hb = h_ref[...].astype(BF16)
    rid_k = lax.broadcasted_iota(jnp.int32, (PEER_N_KEYS, tt), 0).astype(F32)

    k = PEER_TOPK
    strips = [(slice(0, 1), slice(0, k)), (slice(1, 2), slice(0, 8)), (slice(2, 3), slice(0, 8)),
              (slice(3, 4), slice(0, 8)), (slice(0, k), slice(0, 1)), (slice(0, 8), slice(1, 2)),
              (slice(0, 8), slice(2, 3))]

    def grid_rows(x1, x2, scale):
        return jnp.concatenate([x1[sa] * scale + x2[sb] for sa, sb in strips], 0)

    def strip_iota(sa, sb):
        n = max(sa.stop - sa.start, sb.stop - sb.start)
        r = lax.broadcasted_iota(jnp.int32, (n, tt), 0).astype(F32)
        a = r + float(sa.start) if sa.stop - sa.start > 1 else jnp.full((n, tt), float(sa.start), F32)
        b = r + float(sb.start) if sb.stop - sb.start > 1 else jnp.full((n, tt), float(sb.start), F32)
        return a, b

    ab = [strip_iota(sa, sb) for sa, sb in strips]
    rid_c = jnp.concatenate([a * float(k) + b for a, b in ab], 0)
    dup = jnp.concatenate([jnp.where(a < 4.0, -jnp.inf, 0.0) if i >= 4 else jnp.zeros_like(a)
                           for i, (a, b) in enumerate(ab)], 0)

    def head(hd, carry):
        tops = []
        for p in range(2):
            c0 = pl.multiple_of((hd * 2 + p) * PEER_HALF, PEER_HALF)
            q = _dot(hb, wpq_ref[:, pl.ds(c0, PEER_HALF)])
            q = q - jnp.mean(q, -1, keepdims=True)
            q = q * lax.rsqrt(jnp.mean(q * q, -1, keepdims=True) + EPS)
            s = _dot_nt(keys_ref[hd * 2 + p], q.astype(BF16))
            vals, ids = _top_rows(s, rid_k, PEER_N_KEYS)
            tops.append((jnp.concatenate(vals, 0), jnp.concatenate(ids, 0)))
        (s1, i1), (s2, i2) = tops
        cand = grid_rows(s1, s2, 1.0) + dup
        eid = grid_rows(i1, i2, float(PEER_N_KEYS))
        vals, ids = _top_rows(cand, rid_c, PEER_TOPK * PEER_TOPK, payload=eid)
        sc = jnp.concatenate(vals, 0)
        ex = jnp.exp(sc - vals[0])
        r0 = pl.multiple_of(hd * PEER_TOPK, PEER_TOPK)
        gt_ref[pl.ds(r0, PEER_TOPK), :] = ex / jnp.sum(ex, 0, keepdims=True)
        idt_ref[pl.ds(r0, PEER_TOPK), :] = jnp.concatenate(ids, 0)
        return carry

    def head_pair(i, carry):
        head(2 * i, carry)
        return head(2 * i + 1, carry)

    lax.fori_loop(0, PEER_HEADS // 2, head_pair, 0)
    ids_ref[...] = idt_ref[...].T.astype(jnp.int32)


def _peer_route(h, wpq_bf, keys_bf, tt):
    n, d = h.shape
    return pl.pallas_call(
        functools.partial(_peer_route_body, tt=tt),
        grid=(n // tt,),
        in_specs=[pl.BlockSpec((tt, d), lambda i: (i, 0)),
                  pl.BlockSpec(wpq_bf.shape, lambda i: (0, 0)),
                  pl.BlockSpec(keys_bf.shape, lambda i: (0, 0, 0))],
        out_specs=[pl.BlockSpec((tt, PEER_SEL), lambda i: (i, 0)),
                   pl.BlockSpec((PEER_SEL, tt), lambda i: (0, i))],
        out_shape=[jax.ShapeDtypeStruct((n, PEER_SEL), jnp.int32),
                   jax.ShapeDtypeStruct((PEER_SEL, n), F32)],
        scratch_shapes=[pltpu.VMEM((PEER_SEL, tt), F32)],
        compiler_params=_cparams(("arbitrary",)),
        name="peer_route",
    )(h, wpq_bf, keys_bf)


PEER_TOK = 8
SLAB = SUBLANES
PEER_RING = 4
HALF = SUBLANES // 2


def _peer_table(u, v):
    def pack(x):
        bits = lax.bitcast_convert_type(x.astype(BF16), jnp.uint16).astype(jnp.uint32)
        bits = bits.reshape(x.shape[0], SUBLANES, LANES)
        return bits[:, :HALF] | (bits[:, HALF:] << 16)
    return lax.bitcast_convert_type(jnp.concatenate([pack(u), pack(v)], axis=1), jnp.int32)


def _unpack_pair(w):
    lo = pltpu.bitcast(w << 16, F32)
    hi = pltpu.bitcast(w & jnp.int32(-65536), F32)
    return lo, hi


def _peer_expert_body(ids_hbm, tab_hbm, h_ref, gt_ref, o_ref, ids_smem, buf, gsem, isem, *, tg):
    i = pl.program_id(0)
    nsub = tg // PEER_TOK
    total = pl.num_programs(0) * nsub
    last = total - 1
    per_blk = PEER_TOK * PEER_SEL
    lane = lax.broadcasted_iota(jnp.int32, (PEER_SEL, tg), 1)

    assert nsub % PEER_RING == 0

    def ids_copy(blk, slot):
        dst = ids_smem.at[pl.ds(slot * per_blk, per_blk)]
        return pltpu.make_async_copy(ids_hbm.at[jnp.minimum(blk, last)], dst, isem.at[slot])

    def gather(slot, n, priority):
        e = ids_smem[slot * per_blk + n]
        dst = buf.at[pl.ds((slot * per_blk + n) * SLAB, SLAB)]
        pltpu.make_async_copy(tab_hbm.at[e], dst, gsem.at[slot]).start(priority=priority)

    def wait_gathers(slot):
        pltpu.make_async_copy(buf.at[pl.ds(0, per_blk * SLAB)],
                              buf.at[pl.ds(slot * per_blk * SLAB, per_blk * SLAB)], gsem.at[slot]).wait()

    @pl.when(i == 0)
    def _():
        for b in range(PEER_RING):
            ids_copy(b, b).start()
        for b in range(PEER_RING - 1):
            ids_copy(b, b).wait()

            def one(n, c):
                e = ids_smem[b * per_blk + n]
                dst = buf.at[pl.ds(pl.multiple_of((b * per_blk + n) * SLAB, SLAB), SLAB)]
                pltpu.make_async_copy(tab_hbm.at[e], dst, gsem.at[b]).start()
                return c
            lax.fori_loop(0, per_blk, one, 0)

    def batch(g, sb, slot):
        nslot = (slot + PEER_RING - 1) % PEER_RING
        ids_copy(g + PEER_RING - 1, nslot).wait()
        wait_gathers(slot)
        gates = gt_ref[...]
        for t in range(PEER_TOK):
            for j in range(PEER_SEL):
                gather(nslot, t * PEER_SEL + j, j % 2)
            tok = sb * PEER_TOK + t
            hrow = h_ref[tok]
            base = (slot * per_blk + t * PEER_SEL) * SLAB
            part = jnp.zeros((PEER_SEL, LANES), F32)
            for s in range(HALF):
                lo, hi = _unpack_pair(buf[pl.ds(base + s, PEER_SEL, stride=SLAB), :])
                part = part + lo * hrow[s:s + 1, :] + hi * hrow[HALF + s:HALF + s + 1, :]
            act = jnp.sum(part, axis=1, keepdims=True)
            gelu = 0.5 * act * (1.0 + lax.erf(act * (2.0 ** -0.5)))
            gate = jnp.sum(jnp.where(lane == tok, gates, 0.0), axis=1, keepdims=True)
            coef = gate * gelu
            rows_lo, rows_hi = [], []
            for s in range(HALF):
                lo, hi = _unpack_pair(buf[pl.ds(base + HALF + s, PEER_SEL, stride=SLAB), :])
                rows_lo.append(jnp.sum(coef * lo, axis=0, keepdims=True))
                rows_hi.append(jnp.sum(coef * hi, axis=0, keepdims=True))
            o_ref[tok] = jnp.concatenate(rows_lo + rows_hi, axis=0)
        ids_copy(g + PEER_RING, slot).start()

    def group(gi, carry):
        for slot in range(PEER_RING):
            sb = gi * PEER_RING + slot
            batch(i * nsub + sb, sb, slot)
        return carry

    lax.fori_loop(0, nsub // PEER_RING, group, 0)

    @pl.when(i == pl.num_programs(0) - 1)
    def _():
        for slot in range(PEER_RING - 1):
            wait_gathers(slot)
        ids_copy(last + PEER_RING, PEER_RING - 1).wait()


def _peer_expert(h, ids, gt, table, tg):
    n, d = h.shape
    per_blk = PEER_TOK * PEER_SEL
    h3 = h.reshape(n, SUBLANES, LANES)
    out = pl.pallas_call(
        functools.partial(_peer_expert_body, tg=tg),
        grid=(n // tg,),
        in_specs=[pl.BlockSpec(memory_space=pl.ANY), pl.BlockSpec(memory_space=pl.ANY),
                  pl.BlockSpec((tg, SUBLANES, LANES), lambda i: (i, 0, 0)),
                  pl.BlockSpec((PEER_SEL, tg), lambda i: (0, i))],
        out_specs=pl.BlockSpec((tg, SUBLANES, LANES), lambda i: (i, 0, 0)),
        out_shape=jax.ShapeDtypeStruct((n, SUBLANES, LANES), F32),
        scratch_shapes=[pltpu.SMEM((PEER_RING * per_blk,), jnp.int32),
                        pltpu.VMEM((PEER_RING * per_blk * SLAB, LANES), jnp.int32),
                        pltpu.SemaphoreType.DMA((PEER_RING,)), pltpu.SemaphoreType.DMA((PEER_RING,))],
        compiler_params=_cparams(("arbitrary",)),
        name="peer_expert",
    )(ids.reshape(n // PEER_TOK, per_blk), table, h3, gt)
    return out.reshape(n, d)


def _post_ln_body(x_ref, d_ref, g_ref, b_ref, y_ref):
    y_ref[...] = _layer_norm(DN_ALPHA * x_ref[...] + d_ref[...], g_ref[...], b_ref[...])


def _post_ln(x, delta, g, b, tm):
    n, d = x.shape
    row = pl.BlockSpec((tm, d), lambda i: (i, 0))
    full = lambda a: pl.BlockSpec(a.shape, lambda i: (0,) * a.ndim)
    return pl.pallas_call(
        _post_ln_body,
        grid=(n // tm,),
        in_specs=[row, row, full(g), full(b)],
        out_specs=row,
        out_shape=jax.ShapeDtypeStruct((n, d), F32),
        compiler_params=_cparams(("arbitrary",)),
        name="post_ln",
    )(x, delta, g, b)


def _row_tile(n, target):
    t = min(n, target)
    while n % t:
        t -= SUBLANES
    return t


def kernel(x_prompt, x_sample, cache_k, cache_v, state_hgrn, page_table, meta_tokens, emb_ln_g, emb_ln_b, w_in, sb_bias, lb_param, sb_norm_g, hg_norm_g, w_out, ln1_g, ln1_b, w_pq, peer_sub_keys, peer_u, peer_v, ln2_g, ln2_b):
    batch, seq, d = x_prompt.shape
    db, dec_seq = x_sample.shape[0], x_sample.shape[1]
    assert dec_seq == 1 and w_in.shape[0] == DEPTH and lb_param.shape[0] == DEPTH + 1
    n_p = batch * seq
    row2 = lambda a: a.reshape(1, -1).astype(F32)

    w_in_bf = w_in[0].astype(BF16)
    w_out_bf = w_out[0].astype(BF16)
    wpq_bf = w_pq[0].astype(BF16)
    keys_bf = peer_sub_keys[0].reshape(PEER_HEADS * 2, PEER_N_KEYS, PEER_HALF).astype(BF16)
    table = _peer_table(peer_u[0], peer_v[0])
    lbp = lb_param.astype(F32)
    eg, eb = row2(emb_ln_g), row2(emb_ln_b)

    tm = _row_tile(n_p, 256)
    bd_p = _block_tril(tm, jnp.arange(tm) // HG_BLOCK)
    xn_p, q_p, k_p, v_p, qh_p, kh_p, g_p, i_p, gh_p = _ln_inproj(
        x_prompt.reshape(n_p, d), eg, eb, w_in_bf, lbp, bd_p, tm)
    n_s = N_META + db
    x_small = jnp.concatenate([meta_tokens.astype(F32), x_sample.reshape(db, d)], axis=0)
    blocks_s = jnp.concatenate([jnp.zeros((N_META,), jnp.int32), 1 + jnp.arange(db, dtype=jnp.int32)])
    small = _ln_inproj(x_small, eg, eb, w_in_bf, lbp, _block_tril(n_s, blocks_s), n_s)
    k_m, v_m, kh_m, g_m, i_m = (small[j][:N_META] for j in (2, 3, 5, 6, 7))
    xn_s, q_s, k_s, v_s, qh_s, kh_s, lf_s, i_s, gh_s = (a[N_META:] for a in small)

    bias = sb_bias[0].astype(F32)
    o_sb_p = _sb_prompt(q_p, k_p, v_p, k_m, v_m, bias, batch, seq, _row_tile(seq, 256))
    o_hg_p, s_p = _hgrn_prompt(qh_p, kh_p, g_p, i_p, kh_m, g_m, i_m, batch, seq, _row_tile(seq, 512))

    o_sb_s = _sb_decode(q_s, cache_k[0], cache_v[0], page_table, bias)
    o_hg_s, s_s = _hgrn_step(qh_s, kh_s, lf_s, i_s, state_hgrn[0].astype(F32))

    def tail(osb, ohg, gh, xn, tm_rows, tt):
        h1 = _mix_ln(osb, ohg, gh, xn, row2(sb_norm_g[0]), row2(hg_norm_g[0]), w_out_bf,
                     row2(ln1_g[0]), row2(ln1_b[0]), tm_rows)
        ids, gt = _peer_route(h1, wpq_bf, keys_bf, tt)
        po = _peer_expert(h1, ids, gt, table, LANES)
        return _post_ln(h1, po, row2(ln2_g[0]), row2(ln2_b[0]), tm_rows)

    y_p = tail(o_sb_p, o_hg_p, gh_p, xn_p, tm, _row_tile(n_p, 256))
    n_pad = -(-db // LANES) * LANES
    pad = lambda a: jnp.pad(a, ((0, n_pad - db), (0, 0)))
    y_s = tail(pad(o_sb_s), pad(o_hg_s), pad(gh_s), pad(xn_s), n_pad, n_pad)[:db]

    def with_meta(meta_rows, rows):
        m = jnp.broadcast_to(meta_rows[None], (batch, N_META, SB_WIDTH))
        full = jnp.concatenate([m, rows.reshape(batch, seq, SB_WIDTH)], axis=1)
        return full.reshape(1, batch, N_META + seq, SB_HEADS, SB_HEAD_DIM)

    dt = x_prompt.dtype
    return (y_p.reshape(batch, seq, d).astype(dt),
            y_s.reshape(db, 1, d).astype(x_sample.dtype),
            with_meta(k_m, k_p), with_meta(v_m, v_p),
            s_p[None].astype(state_hgrn.dtype),
            k_s.reshape(1, db, 1, SB_HEADS, SB_HEAD_DIM),
            v_s.reshape(1, db, 1, SB_HEADS, SB_HEAD_DIM),
            s_s[None].astype(state_hgrn.dtype))
```

```python
import functools

import jax
import jax.numpy as jnp
from jax import lax
from jax.experimental import pallas as pl
from jax.experimental.pallas import tpu as pltpu

F32 = jnp.float32
BF16 = jnp.bfloat16

N_META = 16
SB_HEADS = 8
SB_HEAD_DIM = 64
SB_WIDTH = SB_HEADS * SB_HEAD_DIM
SB_SCALE = SB_HEAD_DIM ** -0.5
HG_HEADS = 4
HG_DK = 128
HG_DV = 128
HG_W = HG_HEADS * HG_DK
HG_SCALE = HG_DK ** -0.5
HG_BLOCK = 16
PEER_HEADS = 8
PEER_N_KEYS = 128
PEER_HALF = 128
PEER_TOPK = 16
PEER_SEL = PEER_HEADS * PEER_TOPK
DEPTH = 1
DN_ALPHA = (2.0 * DEPTH) ** 0.25
EPS = 1e-5

LANES = 128
SUBLANES = 8
VMEM_LIMIT = 56 * 1024 * 1024


def _cparams(sem):
    return pltpu.CompilerParams(dimension_semantics=sem, vmem_limit_bytes=VMEM_LIMIT)


def _split2(x):
    hi = x.astype(BF16)
    lo = (x - hi.astype(F32)).astype(BF16)
    return hi, lo


def _split3(x):
    hi = x.astype(BF16)
    r = x - hi.astype(F32)
    mid = r.astype(BF16)
    lo = (r - mid.astype(F32)).astype(BF16)
    return hi, mid, lo


def _dot(a, b):
    return jnp.dot(a, b, preferred_element_type=F32)


def _dot_nt(a, b):
    return lax.dot_general(a, b, (((1,), (1,)), ((), ())), preferred_element_type=F32)


def _dot_tn(a, b):
    return lax.dot_general(a, b, (((0,), (0,)), ((), ())), preferred_element_type=F32)


def _sigmoid(x):
    return 1.0 / (1.0 + jnp.exp(-x))


def _layer_norm(x, g, b):
    mu = jnp.mean(x, -1, keepdims=True)
    xc = x - mu
    var = jnp.mean(xc * xc, -1, keepdims=True)
    return xc * lax.rsqrt(var + EPS) * g + b


def _ln_inproj_body(x_ref, g_ref, b_ref, w_ref, lbp_ref, bd_ref,
                    xn_ref, q_ref, k_ref, v_ref, qh_ref, kh_ref, gc_ref, i_ref, gh_ref):
    xn = _layer_norm(x_ref[...], g_ref[...], b_ref[...])
    xn_ref[...] = xn
    xb = xn.astype(BF16)

    def proj(c0, width):
        return _dot(xb, w_ref[:, c0:c0 + width])

    q_ref[...] = proj(0, SB_WIDTH)
    k_ref[...] = proj(SB_WIDTH, SB_WIDTH)
    v_ref[...] = proj(2 * SB_WIDTH, SB_WIDTH)
    c = 3 * SB_WIDTH
    q_raw = proj(c, HG_W)
    f_raw = proj(c + HG_W, HG_W)
    i_ref[...] = proj(c + 2 * HG_W, HG_W)
    gh_ref[...] = proj(c + 3 * HG_W, HG_W)

    p = lbp_ref[...]
    e = jnp.exp(p - jnp.max(p, 0, keepdims=True))
    lb = e[0:1] / jnp.sum(e, 0, keepdims=True)
    log_f = jnp.log(lb + (1.0 - lb) * _sigmoid(f_raw))
    kh_ref[...] = (1.0 - lb) * _sigmoid(-f_raw)
    qh_ref[...] = q_raw * _sigmoid(q_raw) * HG_SCALE
    bd = bd_ref[...]
    hi, mid, lo = _split3(log_f)
    gc_ref[...] = _dot(bd, hi) + _dot(bd, mid) + _dot(bd, lo)


def _ln_inproj(x, g, b, w_bf, lbp, bd, tm):
    n, d = x.shape
    e = w_bf.shape[1]
    row = lambda w: pl.BlockSpec((tm, w), lambda i: (i, 0))
    full = lambda a: pl.BlockSpec(a.shape, lambda i: (0,) * a.ndim)
    outs = [jax.ShapeDtypeStruct((n, d), F32)] + [jax.ShapeDtypeStruct((n, SB_WIDTH), F32)] * 8
    return pl.pallas_call(
        _ln_inproj_body,
        grid=(n // tm,),
        in_specs=[row(d), full(g), full(b), full(w_bf), full(lbp), full(bd)],
        out_specs=[row(d)] + [row(SB_WIDTH)] * 8,
        out_shape=outs,
        compiler_params=_cparams(("arbitrary",)),
        name="ln_inproj",
    )(x, g, b, w_bf, lbp, bd)


def _block_tril(n, blocks):
    i = jnp.arange(n)
    same = blocks[:, None] == blocks[None, :]
    return (same & (i[None, :] <= i[:, None])).astype(BF16)


SB_TILE_W = 256


def _softplus(z):
    return jnp.maximum(z, 0.0) + jnp.log(1.0 + jnp.exp(-jnp.abs(z)))


def _sb_tile(qh, kt, vt, bias, tri, carry, acc, mask):
    z = _dot_nt(qh, kt.astype(BF16)) + bias
    sp_full = _softplus(z)
    sp = sp_full if mask is None else jnp.where(mask, sp_full, 0.0)
    after_in = _dot(sp.astype(BF16), tri)
    w = jnp.exp(z - sp_full - after_in - carry)
    if mask is not None:
        w = jnp.where(mask, w, 0.0)
    acc = acc + _dot(w.astype(BF16), vt.astype(BF16))
    carry = carry + after_in[:, 0:1] + sp[:, 0:1]
    return carry, acc


def _sb_prompt_body(bias_ref, q_ref, k_ref, v_ref, km_ref, vm_ref, tri_ref, o_ref, *, tq):
    hp = pl.program_id(1)
    qi = pl.program_id(2)
    tri = tri_ref[...]
    tri_m = tri_ref[0:N_META, 0:N_META]
    row = lax.broadcasted_iota(jnp.int32, (tq, tq), 0)
    col = lax.broadcasted_iota(jnp.int32, (tq, tq), 1)
    diag_mask = col < row
    d = SB_HEAD_DIM
    heads = [slice(hh * d, (hh + 1) * d) for hh in range(SB_TILE_W // d)]
    biases = [bias_ref[hp * len(heads) + hh] for hh in range(len(heads))]
    qhs = [(q_ref[:, cs] * SB_SCALE).astype(BF16) for cs in heads]

    def step(kref, vref, rows, tri_t, state, mask):
        return tuple(_sb_tile(qh, kref[rows, cs], vref[rows, cs], bias, tri_t, c, a, mask)
                     for qh, cs, bias, (c, a) in zip(qhs, heads, biases, state))

    state = tuple((jnp.zeros((tq, 1), F32), jnp.zeros((tq, d), F32)) for _ in heads)
    state = step(k_ref, v_ref, pl.ds(pl.multiple_of(qi * tq, tq), tq), tri, state, diag_mask)

    def tile_rows(j):
        return pl.ds(pl.multiple_of(j * tq, tq), tq)

    def body2(it, st):
        j = qi - 1 - 2 * it
        st = step(k_ref, v_ref, tile_rows(j), tri, st, None)
        return step(k_ref, v_ref, tile_rows(j - 1), tri, st, None)

    def body1(it, st):
        return step(k_ref, v_ref, tile_rows(0), tri, st, None)

    state = lax.fori_loop(0, qi // 2, body2, state)
    state = lax.fori_loop(0, qi % 2, body1, state)
    state = step(km_ref, vm_ref, slice(None), tri_m, state, None)
    for cs, (_, acc) in zip(heads, state):
        o_ref[:, cs] = acc


def _sb_prompt(q, k, v, km, vm, bias, batch, seq, tq):
    n = q.shape[0]
    nq = seq // tq
    j = jnp.arange(tq)
    tri = (j[:, None] > j[None, :]).astype(BF16)
    qspec = pl.BlockSpec((tq, SB_TILE_W), lambda b, h, i: (b * nq + i, h))
    kspec = pl.BlockSpec((seq, SB_TILE_W), lambda b, h, i: (b, h))
    mspec = pl.BlockSpec((N_META, SB_TILE_W), lambda b, h, i: (0, h))
    return pl.pallas_call(
        functools.partial(_sb_prompt_body, tq=tq),
        grid=(batch, SB_WIDTH // SB_TILE_W, nq),
        in_specs=[pl.BlockSpec(memory_space=pltpu.SMEM), qspec, kspec, kspec, mspec, mspec,
                  pl.BlockSpec((tq, tq), lambda b, h, i: (0, 0))],
        out_specs=qspec,
        out_shape=jax.ShapeDtypeStruct((n, SB_WIDTH), F32),
        compiler_params=_cparams(("arbitrary", "arbitrary", "arbitrary")),
        name="sb_prompt",
    )(bias, q, k, v, km, vm, tri)


DEC_PAGES = 8


def _sb_decode_body(pt_ref, q_ref, *refs):
    kv_refs = refs[:2 * DEC_PAGES]
    bias_ref, seg_ref, segt_ref, trit_ref, o_ref, acc_ref, carry_ref = refs[2 * DEC_PAGES:]
    p = pl.program_id(1)

    @pl.when(p == 0)
    def _():
        acc_ref[...] = jnp.zeros_like(acc_ref)
        carry_ref[...] = jnp.zeros_like(carry_ref)

    qs = q_ref[0] * SB_SCALE
    trit = trit_ref[...]
    acc = acc_ref[...]
    carry = carry_ref[...]
    logw = []
    for r in range(DEC_PAGES):
        z = _dot((kv_refs[r][...] * qs).astype(BF16), seg_ref[...]) + bias_ref[...]
        sp = _softplus(z)
        hi, lo = _split2(sp)
        after_in = _dot(trit, hi) + _dot(trit, lo)
        logw.append((z - sp - after_in, after_in[0:1, :] + sp[0:1, :]))
    for r in range(DEC_PAGES):
        w = jnp.exp(logw[r][0] - carry)
        wexp = _dot(w.astype(BF16), segt_ref[...])
        acc = acc + jnp.sum(wexp * kv_refs[DEC_PAGES + r][...], axis=0, keepdims=True)
        carry = carry + logw[r][1]
    acc_ref[...] = acc
    carry_ref[...] = carry

    @pl.when(p == pl.num_programs(1) - 1)
    def _():
        o_ref[0] = acc


def _sb_decode(q, cache_k, cache_v, page_table, bias):
    db = q.shape[0]
    n_pages = page_table.shape[1]
    n_phys, page = cache_k.shape[0], cache_k.shape[1]
    ck = cache_k.reshape(n_phys, page, SB_WIDTH)
    cv = cache_v.reshape(n_phys, page, SB_WIDTH)
    head_of = jnp.arange(SB_WIDTH) // SB_HEAD_DIM
    seg = (head_of[:, None] == jnp.arange(LANES)[None, :]).astype(BF16)
    segt = seg.T
    j = jnp.arange(page)
    trit = (j[None, :] > j[:, None]).astype(BF16)
    bias_row = jnp.zeros((1, LANES), F32).at[0, :SB_HEADS].set(bias)
    assert n_pages % DEC_PAGES == 0

    def pspec(r):
        return pl.BlockSpec((None, page, SB_WIDTH),
                            lambda b, p, pt: (pt[(b + 1) * n_pages - 1 - (p * DEC_PAGES + r)], 0, 0))

    qspec = pl.BlockSpec((1, 1, SB_WIDTH), lambda b, p, pt: (b, 0, 0))
    full = lambda a: pl.BlockSpec(a.shape, lambda b, p, pt: (0,) * a.ndim)
    pages = [pspec(r) for r in range(DEC_PAGES)]
    out = pl.pallas_call(
        _sb_decode_body,
        grid_spec=pltpu.PrefetchScalarGridSpec(
            num_scalar_prefetch=1,
            grid=(db, n_pages // DEC_PAGES),
            in_specs=[qspec] + pages + pages + [full(bias_row), full(seg), full(segt), full(trit)],
            out_specs=qspec,
            scratch_shapes=[pltpu.VMEM((1, SB_WIDTH), F32), pltpu.VMEM((1, LANES), F32)],
        ),
        out_shape=jax.ShapeDtypeStruct((db, 1, SB_WIDTH), F32),
        compiler_params=_cparams(("arbitrary", "arbitrary")),
        name="sb_decode",
    )(page_table.reshape(-1), q.reshape(db, 1, SB_WIDTH), *([ck] * DEC_PAGES), *([cv] * DEC_PAGES),
      bias_row, seg, segt, trit)
    return out.reshape(db, SB_WIDTH)


def _hgrn_prompt_body(km_ref, gm_ref, vm_ref, q_ref, k_ref, g_ref, v_ref, ones_ref,
                      o_ref, s_ref, st_ref, *, tb):
    t = HG_BLOCK
    step = pl.program_id(1)
    heads = [slice(h * HG_DK, (h + 1) * HG_DK) for h in range(HG_HEADS)]

    @pl.when(step == 0)
    def _():
        for h, cs in enumerate(heads):
            g = gm_ref[:, cs]
            ke = km_ref[:, cs] * jnp.exp(g[t - 1:t, :] - g)
            st_ref[h] = _dot_tn(vm_ref[:, cs].astype(BF16), ke.astype(BF16))

    rowid = lax.broadcasted_iota(jnp.int32, (t, HG_DK), 0)
    ones = ones_ref[...]

    def micro(i, carry):
        r0 = pl.multiple_of(i * t, t)
        for h, cs in enumerate(heads):
            q = q_ref[pl.ds(r0, t), cs]
            k = k_ref[pl.ds(r0, t), cs]
            g = g_ref[pl.ds(r0, t), cs]
            v = v_ref[pl.ds(r0, t), cs]
            st = st_ref[h]
            g_last = g[t - 1:t, :]
            o = _dot_nt((q * jnp.exp(g)).astype(BF16), st.astype(BF16))
            parts = []
            for s in range(t):
                e = jnp.where(rowid >= s, jnp.exp(jnp.minimum(g - g[s:s + 1, :], 0.0)), 0.0)
                parts.append(q * e * k[s:s + 1, :])
            dsum = _dot(jnp.concatenate(parts, axis=0).astype(BF16), ones)
            for s in range(t):
                o = o + dsum[s * t:(s + 1) * t, :] * v[s:s + 1, :]
            o_ref[pl.ds(r0, t), cs] = o
            ke = k * jnp.exp(g_last - g)
            st_ref[h] = st * jnp.exp(g_last) + _dot_tn(v.astype(BF16), ke.astype(BF16))
        return carry

    lax.fori_loop(0, tb // t, micro, 0)

    @pl.when(step == pl.num_programs(1) - 1)
    def _():
        for h in range(HG_HEADS):
            s_ref[0, h] = st_ref[h].T


def _hgrn_prompt(q, k, g, v, km, gm, vm, batch, seq, tb):
    n = q.shape[0]
    nt = seq // tb
    ones = jnp.ones((HG_DK, HG_DV), BF16)
    rspec = pl.BlockSpec((tb, HG_W), lambda b, i: (b * nt + i, 0))
    mspec = pl.BlockSpec((N_META, HG_W), lambda b, i: (0, 0))
    return pl.pallas_call(
        functools.partial(_hgrn_prompt_body, tb=tb),
        grid=(batch, nt),
        in_specs=[mspec, mspec, mspec, rspec, rspec, rspec, rspec,
                  pl.BlockSpec((HG_DK, HG_DV), lambda b, i: (0, 0))],
        out_specs=[rspec, pl.BlockSpec((1, HG_HEADS, HG_DK, HG_DV), lambda b, i: (b, 0, 0, 0))],
        out_shape=[jax.ShapeDtypeStruct((n, HG_W), F32),
                   jax.ShapeDtypeStruct((batch, HG_HEADS, HG_DK, HG_DV), F32)],
        scratch_shapes=[pltpu.VMEM((HG_HEADS, HG_DV, HG_DK), F32)],
        compiler_params=_cparams(("arbitrary", "arbitrary")),
        name="hgrn_prompt",
    )(km, gm, vm, q, k, g, v, ones)


def _hgrn_step_body(q_ref, k_ref, lf_ref, v_ref, s_ref, o_ref, sn_ref):
    for h in range(HG_HEADS):
        sn = jnp.exp(lf_ref[0, h]) * s_ref[0, h] + k_ref[0, h] * v_ref[0, h]
        sn_ref[0, h] = sn
        o_ref[0, h] = jnp.sum(q_ref[0, h] * sn, axis=0, keepdims=True)


def _hgrn_step(q, k, log_f, v, state):
    db = q.shape[0]
    col = lambda a: a.reshape(db, HG_HEADS, HG_DK, 1)
    cspec = pl.BlockSpec((1, HG_HEADS, HG_DK, 1), lambda b: (b, 0, 0, 0))
    vspec = pl.BlockSpec((1, HG_HEADS, 1, HG_DV), lambda b: (b, 0, 0, 0))
    sspec = pl.BlockSpec((1, HG_HEADS, HG_DK, HG_DV), lambda b: (b, 0, 0, 0))
    o, sn = pl.pallas_call(
        _hgrn_step_body,
        grid=(db,),
        in_specs=[cspec, cspec, cspec, vspec, sspec],
        out_specs=[vspec, sspec],
        out_shape=[jax.ShapeDtypeStruct((db, HG_HEADS, 1, HG_DV), F32),
                   jax.ShapeDtypeStruct(state.shape, F32)],
        compiler_params=_cparams(("arbitrary",)),
        name="hgrn_step",
    )(col(q), col(k), col(log_f), v.reshape(db, HG_HEADS, 1, HG_DV), state)
    return o.reshape(db, HG_W), sn


def _mix_ln_body(osb_ref, ohg_ref, gh_ref, xn_ref, sbg_ref, hgg_ref, w_ref, g_ref, b_ref, h_ref):
    osb = osb_ref[...]
    sb = osb * lax.rsqrt(jnp.mean(osb * osb, -1, keepdims=True) + EPS) * sbg_ref[...]
    delta = _dot(sb.astype(BF16), w_ref[0:SB_WIDTH, :])
    for h in range(HG_HEADS):
        cs = slice(h * HG_DV, (h + 1) * HG_DV)
        o = ohg_ref[:, cs]
        gate = gh_ref[:, cs]
        hg = o * lax.rsqrt(jnp.mean(o * o, -1, keepdims=True) + EPS) * hgg_ref[:, cs]
        hg = hg * (gate * _sigmoid(gate))
        delta = delta + _dot(hg.astype(BF16), w_ref[SB_WIDTH + h * HG_DV:SB_WIDTH + (h + 1) * HG_DV, :])
    h_ref[...] = _layer_norm(DN_ALPHA * xn_ref[...] + delta, g_ref[...], b_ref[...])


def _mix_ln(osb, ohg, gh, xn, sbg, hgg, w_bf, g, b, tm):
    n, d = xn.shape
    row = lambda w: pl.BlockSpec((tm, w), lambda i: (i, 0))
    full = lambda a: pl.BlockSpec(a.shape, lambda i: (0,) * a.ndim)
    return pl.pallas_call(
        _mix_ln_body,
        grid=(n // tm,),
        in_specs=[row(SB_WIDTH), row(HG_W), row(HG_W), row(d), full(sbg), full(hgg), full(w_bf),
                  full(g), full(b)],
        out_specs=row(d),
        out_shape=jax.ShapeDtypeStruct((n, d), F32),
        compiler_params=_cparams(("arbitrary",)),
        name="mix_ln",
    )(osb, ohg, gh, xn, sbg, hgg, w_bf, g, b)


def _top_rows(x, rid, n_rows, payload=None):
    vals, ids = [], []
    for _ in range(PEER_TOPK):
        m = jnp.max(x, axis=0, keepdims=True)
        idx = jnp.min(jnp.where(x == m, rid, float(n_rows)), axis=0, keepdims=True)
        hit = rid == idx
        vals.append(m)
        if payload is None:
            ids.append(idx)
        else:
            ids.append(jnp.max(jnp.where(hit, payload, -1.0), axis=0, keepdims=True))
        x = jnp.where(hit, -jnp.inf, x)
    return vals, ids


def _peer_route_body(h_ref, wpq_ref, keys_ref, ids_ref, gt_ref, idt_ref, *, tt):
    hb = h_ref[...].astype(BF16)
    rid_k = lax.broadcasted_iota(jnp.int32, (PEER_N_KEYS, tt), 0).astype(F32)

    k = PEER_TOPK
    strips = [(slice(0, 1), slice(0, k)), (slice(1, 2), slice(0, 8)), (slice(2, 3), slice(0, 8)),
              (slice(3, 4), slice(0, 8)), (slice(0, k), slice(0, 1)), (slice(0, 8), slice(1, 2)),
              (slice(0, 8), slice(2, 3))]

    def grid_rows(x1, x2, scale):
        return jnp.concatenate([x1[sa] * scale + x2[sb] for sa, sb in strips], 0)

    def strip_iota(sa, sb):
        n = max(sa.stop - sa.start, sb.stop - sb.start)
        r = lax.broadcasted_iota(jnp.int32, (n, tt), 0).astype(F32)
        a = r + float(sa.start) if sa.stop - sa.start > 1 else jnp.full((n, tt), float(sa.start), F32)
        b = r + float(sb.start) if sb.stop - sb.start > 1 else jnp.full((n, tt), float(sb.start), F32)
        return a, b

    ab = [strip_iota(sa, sb) for sa, sb in strips]
    rid_c = jnp.concatenate([a * float(k) + b for a, b in ab], 0)
    dup = jnp.concatenate([jnp.where(a < 4.0, -jnp.inf, 0.0) if i >= 4 else jnp.zeros_like(a)
                           for i, (a, b) in enumerate(ab)], 0)

    def head(hd, carry):
        tops = []
        for p in range(2):
            c0 = pl.multiple_of((hd * 2 + p) * PEER_HALF, PEER_HALF)
            q = _dot(hb, wpq_ref[:, pl.ds(c0, PEER_HALF)])
            q = q - jnp.mean(q, -1, keepdims=True)
            q = q * lax.rsqrt(jnp.mean(q * q, -1, keepdims=True) + EPS)
            s = _dot_nt(keys_ref[hd * 2 + p], q.astype(BF16))
            vals, ids = _top_rows(s, rid_k, PEER_N_KEYS)
            tops.append((jnp.concatenate(vals, 0), jnp.concatenate(ids, 0)))
        (s1, i1), (s2, i2) = tops
        cand = grid_rows(s1, s2, 1.0) + dup
        eid = grid_rows(i1, i2, float(PEER_N_KEYS))
        vals, ids = _top_rows(cand, rid_c, PEER_TOPK * PEER_TOPK, payload=eid)
        sc = jnp.concatenate(vals, 0)
        ex = jnp.exp(sc - vals[0])
        r0 = pl.multiple_of(hd * PEER_TOPK, PEER_TOPK)
        gt_ref[pl.ds(r0, PEER_TOPK), :] = ex / jnp.sum(ex, 0, keepdims=True)
        idt_ref[pl.ds(r0, PEER_TOPK), :] = jnp.concatenate(ids, 0)
        return carry

    per_trip = 4

    def head_group(i, carry):
        for r in range(per_trip):
            head(per_trip * i + r, carry)
        return carry

    lax.fori_loop(0, PEER_HEADS // per_trip, head_group, 0)
    ids_ref[...] = idt_ref[...].T.astype(jnp.int32)


def _peer_route(h, wpq_bf, keys_bf, tt):
    n, d = h.shape
    return pl.pallas_call(
        functools.partial(_peer_route_body, tt=tt),
        grid=(n // tt,),
        in_specs=[pl.BlockSpec((tt, d), lambda i: (i, 0)),
                  pl.BlockSpec(wpq_bf.shape, lambda i: (0, 0)),
                  pl.BlockSpec(keys_bf.shape, lambda i: (0, 0, 0))],
        out_specs=[pl.BlockSpec((tt, PEER_SEL), lambda i: (i, 0)),
                   pl.BlockSpec((PEER_SEL, tt), lambda i: (0, i))],
        out_shape=[jax.ShapeDtypeStruct((n, PEER_SEL), jnp.int32),
                   jax.ShapeDtypeStruct((PEER_SEL, n), F32)],
        scratch_shapes=[pltpu.VMEM((PEER_SEL, tt), F32)],
        compiler_params=_cparams(("arbitrary",)),
        name="peer_route",
    )(h, wpq_bf, keys_bf)


PEER_TOK = 8
SLAB = 2 * SUBLANES
PEER_RING = 4


def _peer_table(u, v):
    n_exp = u.shape[0]
    return jnp.concatenate([u.reshape(n_exp, SUBLANES, LANES).astype(F32),
                            v.reshape(n_exp, SUBLANES, LANES).astype(F32)], axis=1)


def _peer_expert_body(ids_hbm, tab_hbm, h_ref, gt_ref, o_ref, ids_smem, buf, gsem, isem, *, tg):
    i = pl.program_id(0)
    nsub = tg // PEER_TOK
    total = pl.num_programs(0) * nsub
    last = total - 1
    per_blk = PEER_TOK * PEER_SEL
    lane = lax.broadcasted_iota(jnp.int32, (PEER_SEL, tg), 1)

    assert nsub % PEER_RING == 0

    def ids_copy(blk, slot):
        dst = ids_smem.at[pl.ds(slot * per_blk, per_blk)]
        return pltpu.make_async_copy(ids_hbm.at[jnp.minimum(blk, last)], dst, isem.at[slot])

    def gather(slot, n, priority):
        e = ids_smem[slot * per_blk + n]
        dst = buf.at[pl.ds((slot * per_blk + n) * SLAB, SLAB)]
        pltpu.make_async_copy(tab_hbm.at[e], dst, gsem.at[slot]).start(priority=priority)

    def wait_gathers(slot):
        pltpu.make_async_copy(buf.at[pl.ds(0, per_blk * SLAB)],
                              buf.at[pl.ds(slot * per_blk * SLAB, per_blk * SLAB)], gsem.at[slot]).wait()

    @pl.when(i == 0)
    def _():
        for b in range(PEER_RING):
            ids_copy(b, b).start()
        for b in range(PEER_RING - 1):
            ids_copy(b, b).wait()

            def one(n, c):
                e = ids_smem[b * per_blk + n]
                dst = buf.at[pl.ds(pl.multiple_of((b * per_blk + n) * SLAB, SLAB), SLAB)]
                pltpu.make_async_copy(tab_hbm.at[e], dst, gsem.at[b]).start()
                return c
            lax.fori_loop(0, per_blk, one, 0)

    def batch(g, sb, slot):
        nslot = (slot + PEER_RING - 1) % PEER_RING
        ids_copy(g + PEER_RING - 1, nslot).wait()
        wait_gathers(slot)
        gates = gt_ref[...]
        for t in range(PEER_TOK):
            for j in range(PEER_SEL):
                gather(nslot, t * PEER_SEL + j, j % 2)
            tok = sb * PEER_TOK + t
            hrow = h_ref[tok]
            base = (slot * per_blk + t * PEER_SEL) * SLAB
            part = jnp.zeros((PEER_SEL, LANES), F32)
            for c in range(SUBLANES):
                part = part + buf[pl.ds(base + c, PEER_SEL, stride=SLAB), :] * hrow[c:c + 1, :]
            act = jnp.sum(part, axis=1, keepdims=True)
            gelu = 0.5 * act * (1.0 + lax.erf(act * (2.0 ** -0.5)))
            gate = jnp.sum(jnp.where(lane == tok, gates, 0.0), axis=1, keepdims=True)
            coef = gate * gelu
            rows = []
            for c in range(SUBLANES):
                v_c = buf[pl.ds(base + SUBLANES + c, PEER_SEL, stride=SLAB), :]
                rows.append(jnp.sum(coef * v_c, axis=0, keepdims=True))
            o_ref[tok] = jnp.concatenate(rows, axis=0)
        ids_copy(g + PEER_RING, slot).start()

    def group(gi, carry):
        for slot in range(PEER_RING):
            sb = gi * PEER_RING + slot
            batch(i * nsub + sb, sb, slot)
        return carry

    lax.fori_loop(0, nsub // PEER_RING, group, 0)

    @pl.when(i == pl.num_programs(0) - 1)
    def _():
        for slot in range(PEER_RING - 1):
            wait_gathers(slot)
        ids_copy(last + PEER_RING, PEER_RING - 1).wait()


def _peer_expert(h, ids, gt, table, tg):
    n, d = h.shape
    per_blk = PEER_TOK * PEER_SEL
    h3 = h.reshape(n, SUBLANES, LANES)
    out = pl.pallas_call(
        functools.partial(_peer_expert_body, tg=tg),
        grid=(n // tg,),
        in_specs=[pl.BlockSpec(memory_space=pl.ANY), pl.BlockSpec(memory_space=pl.ANY),
                  pl.BlockSpec((tg, SUBLANES, LANES), lambda i: (i, 0, 0)),
                  pl.BlockSpec((PEER_SEL, tg), lambda i: (0, i))],
        out_specs=pl.BlockSpec((tg, SUBLANES, LANES), lambda i: (i, 0, 0)),
        out_shape=jax.ShapeDtypeStruct((n, SUBLANES, LANES), F32),
        scratch_shapes=[pltpu.SMEM((PEER_RING * per_blk,), jnp.int32),
                        pltpu.VMEM((PEER_RING * per_blk * SLAB, LANES), F32),
                        pltpu.SemaphoreType.DMA((PEER_RING,)), pltpu.SemaphoreType.DMA((PEER_RING,))],
        compiler_params=_cparams(("arbitrary",)),
        name="peer_expert",
    )(ids.reshape(n // PEER_TOK, per_blk), table, h3, gt)
    return out.reshape(n, d)


def _post_ln_body(x_ref, d_ref, g_ref, b_ref, y_ref):
    y_ref[...] = _layer_norm(DN_ALPHA * x_ref[...] + d_ref[...], g_ref[...], b_ref[...])


def _post_ln(x, delta, g, b, tm):
    n, d = x.shape
    row = pl.BlockSpec((tm, d), lambda i: (i, 0))
    full = lambda a: pl.BlockSpec(a.shape, lambda i: (0,) * a.ndim)
    return pl.pallas_call(
        _post_ln_body,
        grid=(n // tm,),
        in_specs=[row, row, full(g), full(b)],
        out_specs=row,
        out_shape=jax.ShapeDtypeStruct((n, d), F32),
        compiler_params=_cparams(("arbitrary",)),
        name="post_ln",
    )(x, delta, g, b)


def _row_tile(n, target):
    t = min(n, target)
    while n % t:
        t -= SUBLANES
    return t


def kernel(x_prompt, x_sample, cache_k, cache_v, state_hgrn, page_table, meta_tokens, emb_ln_g, emb_ln_b, w_in, sb_bias, lb_param, sb_norm_g, hg_norm_g, w_out, ln1_g, ln1_b, w_pq, peer_sub_keys, peer_u, peer_v, ln2_g, ln2_b):
    batch, seq, d = x_prompt.shape
    db, dec_seq = x_sample.shape[0], x_sample.shape[1]
    assert dec_seq == 1 and w_in.shape[0] == DEPTH and lb_param.shape[0] == DEPTH + 1
    n_p = batch * seq
    row2 = lambda a: a.reshape(1, -1).astype(F32)

    w_in_bf = w_in[0].astype(BF16)
    w_out_bf = w_out[0].astype(BF16)
    wpq_bf = w_pq[0].astype(BF16)
    keys_bf = peer_sub_keys[0].reshape(PEER_HEADS * 2, PEER_N_KEYS, PEER_HALF).astype(BF16)
    table = _peer_table(peer_u[0], peer_v[0])
    lbp = lb_param.astype(F32)
    eg, eb = row2(emb_ln_g), row2(emb_ln_b)

    tm = _row_tile(n_p, 256)
    bd_p = _block_tril(tm, jnp.arange(tm) // HG_BLOCK)
    xn_p, q_p, k_p, v_p, qh_p, kh_p, g_p, i_p, gh_p = _ln_inproj(
        x_prompt.reshape(n_p, d), eg, eb, w_in_bf, lbp, bd_p, tm)
    n_s = N_META + db
    x_small = jnp.concatenate([meta_tokens.astype(F32), x_sample.reshape(db, d)], axis=0)
    blocks_s = jnp.concatenate([jnp.zeros((N_META,), jnp.int32), 1 + jnp.arange(db, dtype=jnp.int32)])
    small = _ln_inproj(x_small, eg, eb, w_in_bf, lbp, _block_tril(n_s, blocks_s), n_s)
    k_m, v_m, kh_m, g_m, i_m = (small[j][:N_META] for j in (2, 3, 5, 6, 7))
    xn_s, q_s, k_s, v_s, qh_s, kh_s, lf_s, i_s, gh_s = (a[N_META:] for a in small)

    bias = sb_bias[0].astype(F32)
    o_sb_p = _sb_prompt(q_p, k_p, v_p, k_m, v_m, bias, batch, seq, _row_tile(seq, 256))
    o_hg_p, s_p = _hgrn_prompt(qh_p, kh_p, g_p, i_p, kh_m, g_m, i_m, batch, seq, _row_tile(seq, 512))

    o_sb_s = _sb_decode(q_s, cache_k[0], cache_v[0], page_table, bias)
    o_hg_s, s_s = _hgrn_step(qh_s, kh_s, lf_s, i_s, state_hgrn[0].astype(F32))

    def tail(osb, ohg, gh, xn, tm_rows, tt):
        h1 = _mix_ln(osb, ohg, gh, xn, row2(sb_norm_g[0]), row2(hg_norm_g[0]), w_out_bf,
                     row2(ln1_g[0]), row2(ln1_b[0]), tm_rows)
        ids, gt = _peer_route(h1, wpq_bf, keys_bf, tt)
        po = _peer_expert(h1, ids, gt, table, LANES)
        return _post_ln(h1, po, row2(ln2_g[0]), row2(ln2_b[0]), tm_rows)

    y_p = tail(o_sb_p, o_hg_p, gh_p, xn_p, tm, _row_tile(n_p, 256))
    n_pad = -(-db // LANES) * LANES
    pad = lambda a: jnp.pad(a, ((0, n_pad - db), (0, 0)))
    y_s = tail(pad(o_sb_s), pad(o_hg_s), pad(gh_s), pad(xn_s), n_pad, n_pad)[:db]

    def with_meta(meta_rows, rows):
        m = jnp.broadcast_to(meta_rows[None], (batch, N_META, SB_WIDTH))
        full = jnp.concatenate([m, rows.reshape(batch, seq, SB_WIDTH)], axis=1)
        return full.reshape(1, batch, N_META + seq, SB_HEADS, SB_HEAD_DIM)

    dt = x_prompt.dtype
    return (y_p.reshape(batch, seq, d).astype(dt),
            y_s.reshape(db, 1, d).astype(x_sample.dtype),
            with_meta(k_m, k_p), with_meta(v_m, v_p),
            s_p[None].astype(state_hgrn.dtype),
            k_s.reshape(1, db, 1, SB_HEADS, SB_HEAD_DIM),
            v_s.reshape(1, db, 1, SB_HEADS, SB_HEAD_DIM),
            s_s[None].astype(state_hgrn.dtype))
```
